```python
import jax, jax.numpy as jnp
from jax import lax
import numpy as np

D_MODEL = 2048
BATCH = 16
SEQ = 2048
DEPTH = 1

ATT_WIDTH = D_MODEL // 2
ATT_HEAD_DIM = 64
ATT_HEADS = ATT_WIDTH // ATT_HEAD_DIM
ATT_KV_HEADS = 4
ATT_KV_WIDTH = ATT_KV_HEADS * ATT_HEAD_DIM
WINDOW = 128
ATT_BLOCK = 128
ML_WIDTH = D_MODEL // 2
ML_HEADS = 4
ML_HEAD_DIM = ML_WIDTH // ML_HEADS
ML_CHUNK = 64
CONV_WIDTH = 4
F_BIAS_LO = 3.0
F_BIAS_HI = 6.0
N_BRANCHES = 2
PEER_HEADS = 8
N_KEYS = 128
N_EXPERTS = N_KEYS * N_KEYS
PEER_KEY_DIM = 128
PEER_TOPK_HALF = 16
PEER_TOPK = 16
PEER_TOKEN_BLOCK = 128

EPS = 1e-6
NEG_INF = -1e30

IN_SIZES = (ATT_WIDTH, ATT_KV_WIDTH, ATT_KV_WIDTH,
            ML_WIDTH, ML_WIDTH, ML_WIDTH, ML_WIDTH, ML_HEADS, ML_HEADS,
            N_BRANCHES * D_MODEL)
IN_WIDTH = sum(IN_SIZES)
F_OFFSET = ATT_WIDTH + 2 * ATT_KV_WIDTH + 4 * ML_WIDTH + ML_HEADS

kernel_name = "hybrid_swa_mlstm_peer_block"


def rms_norm(x, g):
    xf = x.astype(jnp.float32)
    y = xf * lax.rsqrt(jnp.mean(xf * xf, axis=-1, keepdims=True) + EPS)
    return (y * g.astype(jnp.float32)).astype(x.dtype)


def causal_depthwise_conv(x, w, b):
    C = x.shape[-1]
    y = lax.conv_general_dilated(x, w[:, None, :].astype(x.dtype), window_strides=(1,),
                                 padding=[(CONV_WIDTH - 1, 0)],
                                 dimension_numbers=('NWC', 'WIO', 'NWC'),
                                 feature_group_count=C)
    return y + b


def sliding_window_attention(q, k, v, sinks):
    B, S, _, dh = q.shape
    nb = S // ATT_BLOCK
    G = ATT_HEADS // ATT_KV_HEADS
    qb = q.reshape(B, nb, ATT_BLOCK, ATT_KV_HEADS, G, dh).swapaxes(0, 1)

    def band(a):
        ap = jnp.pad(a, ((0, 0), (ATT_BLOCK, 0), (0, 0), (0, 0)))
        ap = ap.reshape(B, nb + 1, ATT_BLOCK, ATT_KV_HEADS, dh)
        return jnp.concatenate([ap[:, :-1], ap[:, 1:]], axis=2).swapaxes(0, 1)

    kb, vb = band(k), band(v)
    q_pos = jnp.arange(ATT_BLOCK)[:, None]
    k_pos = jnp.arange(2 * ATT_BLOCK)[None, :] - ATT_BLOCK
    diff = q_pos - k_pos
    blk_start = jnp.arange(nb)[:, None, None] * ATT_BLOCK
    valid = (diff >= 0) & (diff < WINDOW) & (blk_start + k_pos >= 0)
    sink = sinks.astype(jnp.float32).reshape(ATT_KV_HEADS, G)[None, :, :, None, None]
    scale = dh ** -0.5

    def one_block(args):
        qi, ki, vi, mi = args
        s = jnp.einsum('bqhgd,bkhd->bhgqk', qi.astype(jnp.float32), ki.astype(jnp.float32)) * scale
        s = jnp.where(mi[None, None, None], s, NEG_INF)
        m = jnp.maximum(s.max(axis=-1, keepdims=True), sink)
        p = jnp.exp(s - m)
        p = p / (p.sum(axis=-1, keepdims=True) + jnp.exp(sink - m))
        return jnp.einsum('bhgqk,bkhd->bqhgd', p, vi.astype(jnp.float32))

    o = lax.map(one_block, (qb, kb, vb, valid))
    return o.swapaxes(0, 1).reshape(B, S, ATT_WIDTH).astype(q.dtype)


def mlstm_chunkwise(q, k, v, i_pre, f_pre):
    B, S, H, d = q.shape
    L = ML_CHUNK
    nc = S // L
    q = q.astype(jnp.float32)
    k = k.astype(jnp.float32) * (d ** -0.5)
    v = v.astype(jnp.float32)
    i_pre = i_pre.astype(jnp.float32)
    log_f = jax.nn.log_sigmoid(f_pre.astype(jnp.float32))

    def to_chunks(a):
        a = a.reshape(B, nc, L, H, *a.shape[3:])
        return jnp.moveaxis(a, (1, 3), (0, 2))

    tri = jnp.tril(jnp.ones((L, L), dtype=bool))

    def step(carry, xs):
        C, n, m = carry
        qc, kc, vc, ic, lfc = xs
        b = jnp.cumsum(lfc, axis=-1)
        logD = jnp.where(tri, b[..., :, None] - b[..., None, :] + ic[..., None, :], NEG_INF)
        m_inter = b + m[..., None]
        m_t = jnp.maximum(logD.max(axis=-1), m_inter)
        Dm = jnp.exp(logD - m_t[..., None])
        Sc = jnp.einsum('bhtd,bhsd->bhts', qc, kc) * Dm
        w_inter = jnp.exp(m_inter - m_t)
        num = (jnp.einsum('bhts,bhsd->bhtd', Sc, vc)
               + w_inter[..., None] * jnp.einsum('bhtk,bhvk->bhtv', qc, C))
        den = Sc.sum(axis=-1) + w_inter * jnp.einsum('bhtk,bhk->bht', qc, n)
        h = num / jnp.maximum(jnp.abs(den), jnp.exp(-m_t))[..., None]
        bL = b[..., -1]
        log_w = bL[..., None] - b + ic
        m_new = jnp.maximum(bL + m, log_w.max(axis=-1))
        w = jnp.exp(log_w - m_new[..., None])
        decay = jnp.exp(bL + m - m_new)
        C_new = decay[..., None, None] * C + jnp.einsum('bhs,bhsv,bhsk->bhvk', w, vc, kc)
        n_new = decay[..., None] * n + jnp.einsum('bhs,bhsk->bhk', w, kc)
        return (C_new, n_new, m_new), h

    init = (jnp.zeros((B, H, d, d), jnp.float32), jnp.zeros((B, H, d), jnp.float32),
            jnp.full((B, H), NEG_INF, jnp.float32))
    _, hs = lax.scan(step, init, (to_chunks(q), to_chunks(k), to_chunks(v),
                                  to_chunks(i_pre), to_chunks(log_f)))
    return jnp.moveaxis(hs, (0, 2), (1, 3)).reshape(B, S, H, d)


def peer(xn, w_peer_q, peer_keys, peer_u, peer_v):
    B, S, D = xn.shape
    T = B * S
    xt = xn.reshape(T, D)
    qp = (xt @ w_peer_q).reshape(T, PEER_HEADS, 2, PEER_KEY_DIM)
    scores = jnp.einsum('thcd,hcnd->thcn', qp.astype(jnp.float32), peer_keys.astype(jnp.float32))
    s_top, i_top = lax.top_k(scores, PEER_TOPK_HALF)
    cand = (s_top[:, :, 0, :, None] + s_top[:, :, 1, None, :]).reshape(T, PEER_HEADS, -1)
    cand_idx = (i_top[:, :, 0, :, None] * N_KEYS + i_top[:, :, 1, None, :]).reshape(T, PEER_HEADS, -1)
    top_s, pos = lax.top_k(cand, PEER_TOPK)
    idx = jnp.take_along_axis(cand_idx, pos, axis=-1)
    gates = jax.nn.softmax(top_s, axis=-1).astype(xn.dtype)
    nblk = T // PEER_TOKEN_BLOCK

    def block(args):
        xb, ib, gb = args
        u = peer_u[ib]
        act = jax.nn.gelu(jnp.einsum('td,thkd->thk', xb, u), approximate=False)
        return jnp.einsum('thk,thkd->td', gb * act, peer_v[ib])

    out = lax.map(block, (xt.reshape(nblk, PEER_TOKEN_BLOCK, D),
                          idx.reshape(nblk, PEER_TOKEN_BLOCK, PEER_HEADS, PEER_TOPK),
                          gates.reshape(nblk, PEER_TOKEN_BLOCK, PEER_HEADS, PEER_TOPK)))
    return out.reshape(B, S, D)


def hybrid_layer(x, norm_mix_g, w_in, b_in, conv_w, conv_b, q_norm_g, k_norm_g, sinks,
                 ml_norm_g, w_proj_att, w_proj_ml, w_out, norm_ffn_g, w_peer_q,
                 peer_keys, peer_u, peer_v):
    B, S, _ = x.shape
    xn = rms_norm(x, norm_mix_g)
    proj = xn @ w_in + b_in
    split_idx = [int(c) for c in np.cumsum(IN_SIZES)[:-1]]
    a_q, a_k, a_v, m_q, m_k, m_v, m_o, m_i, m_f, g_pre = jnp.split(proj, split_idx, axis=-1)
    a_q = rms_norm(a_q.reshape(B, S, ATT_HEADS, ATT_HEAD_DIM), q_norm_g)
    a_k = rms_norm(a_k.reshape(B, S, ATT_KV_HEADS, ATT_HEAD_DIM), k_norm_g)
    a_v = a_v.reshape(B, S, ATT_KV_HEADS, ATT_HEAD_DIM)
    att = sliding_window_attention(a_q, a_k, a_v, sinks)
    qk = jax.nn.silu(causal_depthwise_conv(jnp.concatenate([m_q, m_k], axis=-1), conv_w, conv_b))
    m_q, m_k = jnp.split(qk, 2, axis=-1)
    h = mlstm_chunkwise(m_q.reshape(B, S, ML_HEADS, ML_HEAD_DIM),
                        m_k.reshape(B, S, ML_HEADS, ML_HEAD_DIM),
                        m_v.reshape(B, S, ML_HEADS, ML_HEAD_DIM), m_i, m_f)
    h = rms_norm(h, ml_norm_g.reshape(ML_HEADS, ML_HEAD_DIM)).reshape(B, S, ML_WIDTH).astype(x.dtype)
    ml = jax.nn.sigmoid(m_o) * h
    g = jax.nn.sigmoid(g_pre).reshape(B, S, N_BRANCHES, D_MODEL)
    merged = g[:, :, 0] * (att @ w_proj_att) + g[:, :, 1] * (ml @ w_proj_ml)
    x = x + merged @ w_out
    x = x + peer(rms_norm(x, norm_ffn_g), w_peer_q, peer_keys, peer_u, peer_v)
    return x


def setup_inputs(seed: int = 0) -> dict:
    key = jax.random.key(seed)
    ks = jax.random.split(key, 20)
    f32 = jnp.float32

    def nrm(k, shape, scale):
        return jax.random.normal(k, shape, f32) * scale

    Dp = DEPTH
    b_in = nrm(ks[3], (Dp, IN_WIDTH), 0.02)
    b_in = b_in.at[:, F_OFFSET:F_OFFSET + ML_HEADS].add(jnp.linspace(F_BIAS_LO, F_BIAS_HI, ML_HEADS))
    return {
        "x": nrm(ks[0], (BATCH, SEQ, D_MODEL), 1.0),
        "norm_mix_g": 1.0 + nrm(ks[1], (Dp, D_MODEL), 0.02),
        "w_in": nrm(ks[2], (Dp, D_MODEL, IN_WIDTH), D_MODEL ** -0.5),
        "b_in": b_in,
        "conv_w": nrm(ks[4], (Dp, CONV_WIDTH, 2 * ML_WIDTH), CONV_WIDTH ** -0.5),
        "conv_b": nrm(ks[5], (Dp, 2 * ML_WIDTH), 0.02),
        "q_norm_g": 1.0 + nrm(ks[6], (Dp, ATT_HEAD_DIM), 0.02),
        "k_norm_g": 1.0 + nrm(ks[7], (Dp, ATT_HEAD_DIM), 0.02),
        "sinks": nrm(ks[8], (Dp, ATT_HEADS), 0.5),
        "ml_norm_g": 1.0 + nrm(ks[9], (Dp, ML_WIDTH), 0.02),
        "w_proj_att": nrm(ks[10], (Dp, ATT_WIDTH, D_MODEL), ATT_WIDTH ** -0.5),
        "w_proj_ml": nrm(ks[11], (Dp, ML_WIDTH, D_MODEL), ML_WIDTH ** -0.5),
        "w_out": nrm(ks[12], (Dp, D_MODEL, D_MODEL), D_MODEL ** -0.5),
        "norm_ffn_g": 1.0 + nrm(ks[13], (Dp, D_MODEL), 0.02),
        "w_peer_q": nrm(ks[14], (Dp, D_MODEL, PEER_HEADS * 2 * PEER_KEY_DIM), D_MODEL ** -0.5),
        "peer_keys": nrm(ks[15], (Dp, PEER_HEADS, 2, N_KEYS, PEER_KEY_DIM), PEER_KEY_DIM ** -0.5),
        "peer_u": nrm(ks[16], (Dp, N_EXPERTS, D_MODEL), D_MODEL ** -0.5),
        "peer_v": nrm(ks[17], (Dp, N_EXPERTS, D_MODEL), PEER_HEADS ** -0.5),
    }


def reference(x, norm_mix_g, w_in, b_in, conv_w, conv_b, q_norm_g, k_norm_g, sinks,
              ml_norm_g, w_proj_att, w_proj_ml, w_out, norm_ffn_g, w_peer_q,
              peer_keys, peer_u, peer_v):
    for l in range(DEPTH):
        x = hybrid_layer(x, norm_mix_g[l], w_in[l], b_in[l], conv_w[l], conv_b[l],
                         q_norm_g[l], k_norm_g[l], sinks[l], ml_norm_g[l],
                         w_proj_att[l], w_proj_ml[l], w_out[l], norm_ffn_g[l],
                         w_peer_q[l], peer_keys[l], peer_u[l], peer_v[l])
    return x
```

```python
import functools

import jax
import jax.numpy as jnp
from jax import lax
from jax.experimental import pallas as pl
from jax.experimental.pallas import tpu as pltpu

F32 = jnp.float32
BF16 = jnp.bfloat16

D_MODEL = 2048
ATT_WIDTH = 1024
ATT_HEAD_DIM = 64
ATT_HEADS = 16
ATT_KV_HEADS = 4
ATT_GROUP = ATT_HEADS // ATT_KV_HEADS
ATT_KV_WIDTH = ATT_KV_HEADS * ATT_HEAD_DIM
ATT_BLOCK = 128
ML_WIDTH = 1024
ML_HEADS = 4
ML_HEAD_DIM = 256
CONV_WIDTH = 4
PEER_HEADS = 8
N_KEYS = 128
N_EXPERTS = N_KEYS * N_KEYS
PEER_KEY_DIM = 128
PEER_TOPK = 16
EPS = 1e-6
NEG_INF = -1e30

LANES = 128
SUBLANES = 8
VMEM_LIMIT = 56 * 1024 * 1024

COL_G = 0
COL_AQ = COL_G + 2 * D_MODEL
COL_MQ = COL_AQ + ATT_WIDTH
COL_MK = COL_MQ + ML_WIDTH
COL_MV = COL_MK + ML_WIDTH
COL_MO = COL_MV + ML_WIDTH
COL_AK = COL_MO + ML_WIDTH
COL_AV = COL_AK + ATT_KV_WIDTH
PACKED_WIDTH = COL_AV + ATT_KV_WIDTH
SRC_AQ = 0
SRC_AK = SRC_AQ + ATT_WIDTH
SRC_AV = SRC_AK + ATT_KV_WIDTH
SRC_MQ = SRC_AV + ATT_KV_WIDTH
SRC_MK = SRC_MQ + ML_WIDTH
SRC_MV = SRC_MK + ML_WIDTH
SRC_MO = SRC_MV + ML_WIDTH
SRC_MI = SRC_MO + ML_WIDTH
SRC_MF = SRC_MI + ML_HEADS
SRC_G = SRC_MF + ML_HEADS

ML_CHUNK = 256
ML_AUG = ML_HEAD_DIM + LANES


def _dot(a, b):
    return jnp.dot(a, b, preferred_element_type=F32)


def _dot_nt(a, b):
    return lax.dot_general(a, b, (((1,), (1,)), ((), ())), preferred_element_type=F32)


def _dot_tn(a, b):
    return lax.dot_general(a, b, (((0,), (0,)), ((), ())), preferred_element_type=F32)


def _split_bf16(a):
    hi = a.astype(BF16)
    lo = (a - hi.astype(F32)).astype(BF16)
    return hi, lo


def _params(*sem):
    return pltpu.CompilerParams(dimension_semantics=sem, vmem_limit_bytes=VMEM_LIMIT)


def _resident(shape, index_map):
    return pl.BlockSpec(shape, index_map, pipeline_mode=pl.Buffered(1))


def _inproj_kernel(x_ref, g_ref, w_ref, b_ref, wif_ref, bif_ref, wift_ref, bift_ref,
                   o_ref, ifc_ref, ift_ref, xn_ref):
    @pl.when(pl.program_id(1) == 0)
    def _():
        x = x_ref[...]
        xn = x * lax.rsqrt(jnp.mean(x * x, axis=-1, keepdims=True) + EPS) * g_ref[...]
        xn_ref[...] = xn.astype(BF16)
        xh, xl = _split_bf16(xn)
        wh, wl = _split_bf16(wif_ref[...])
        ifc_ref[...] = _dot(xh, wh) + _dot(xh, wl) + _dot(xl, wh) + bif_ref[...]
        th, tl = _split_bf16(wift_ref[...])
        ift_ref[...] = _dot_nt(th, xh) + _dot_nt(th, xl) + _dot_nt(tl, xh) + bift_ref[...]

    o_ref[...] = (_dot(xn_ref[...], w_ref[...]) + b_ref[...]).astype(o_ref.dtype)


def _inproj(x2, g, w_packed, b_packed, wif, bif, wift, bift, tm, tn):
    T = x2.shape[0]
    grid = (T // tm, PACKED_WIDTH // tn)
    return pl.pallas_call(
        _inproj_kernel,
        grid=grid,
        in_specs=[
            pl.BlockSpec((tm, D_MODEL), lambda i, j: (i, 0)),
            pl.BlockSpec((1, D_MODEL), lambda i, j: (0, 0)),
            pl.BlockSpec((D_MODEL, tn), lambda i, j: (0, j)),
            pl.BlockSpec((1, tn), lambda i, j: (0, j)),
            pl.BlockSpec((D_MODEL, LANES), lambda i, j: (0, 0)),
            pl.BlockSpec((1, LANES), lambda i, j: (0, 0)),
            pl.BlockSpec((SUBLANES, D_MODEL), lambda i, j: (0, 0)),
            pl.BlockSpec((SUBLANES, 1), lambda i, j: (0, 0)),
        ],
        out_specs=[
            pl.BlockSpec((tm, tn), lambda i, j: (i, j)),
            pl.BlockSpec((tm, LANES), lambda i, j: (i, 0)),
            pl.BlockSpec((SUBLANES, tm), lambda i, j: (0, i)),
        ],
        out_shape=[
            jax.ShapeDtypeStruct((T, PACKED_WIDTH), BF16),
            jax.ShapeDtypeStruct((T, LANES), F32),
            jax.ShapeDtypeStruct((SUBLANES, T), F32),
        ],
        scratch_shapes=[pltpu.VMEM((tm, D_MODEL), BF16)],
        compiler_params=_params("parallel", "arbitrary"),
        name="inproj",
    )(x2, g, w_packed, b_packed, wif, bif, wift, bift)


def _head_norm(a, g):
    return a * lax.rsqrt(jnp.mean(a * a, axis=-1, keepdims=True) + EPS) * g


def _attn_kernel(sinks_ref, q_ref, kc_ref, kp_ref, vc_ref, vp_ref, qg_ref, kg_ref, o_ref):
    j = pl.program_id(1)
    blk = ATT_BLOCK
    qpos = lax.broadcasted_iota(jnp.int32, (blk, 2 * blk), 0)
    kpos = lax.broadcasted_iota(jnp.int32, (blk, 2 * blk), 1)
    is_prev = kpos < blk
    no_prev = jnp.where(j > 0, 0, 2 * blk)
    valid = (jnp.where(is_prev, kpos - no_prev, qpos + blk) >= jnp.where(is_prev, qpos + 1, kpos))
    scale = ATT_HEAD_DIM ** -0.5
    qg = qg_ref[...]
    kg = kg_ref[...]
    q = q_ref[...].astype(F32)
    outs = []
    for hk in range(ATT_KV_HEADS):
        cs = slice(hk * ATT_HEAD_DIM, (hk + 1) * ATT_HEAD_DIM)
        k = jnp.concatenate([kp_ref[:, cs], kc_ref[:, cs]], axis=0).astype(F32)
        kn = _head_norm(k, kg).astype(BF16)
        v = jnp.concatenate([vp_ref[:, cs], vc_ref[:, cs]], axis=0)
        for g in range(ATT_GROUP):
            h = hk * ATT_GROUP + g
            qh = q[:, h * ATT_HEAD_DIM:(h + 1) * ATT_HEAD_DIM]
            qn = (_head_norm(qh, qg) * scale).astype(BF16)
            s = jnp.where(valid, _dot_nt(qn, kn), NEG_INF)
            sink = sinks_ref[h]
            m = jnp.maximum(jnp.max(s, axis=-1, keepdims=True), sink)
            p = jnp.exp(s - m)
            den = jnp.sum(p, axis=-1, keepdims=True) + jnp.exp(sink - m)
            outs.append(_dot(p.astype(BF16), v) / den)
    o_ref[...] = jnp.concatenate(outs, axis=-1).astype(o_ref.dtype)


def _attention(proj, sinks, qg, kg, B, S):
    nb = S // ATT_BLOCK
    T = B * S
    kcol = COL_AK // ATT_KV_WIDTH
    vcol = COL_AV // ATT_KV_WIDTH
    cur = lambda b, j: b * nb + j
    prev = lambda b, j: b * nb + jnp.maximum(j - 1, 0)
    return pl.pallas_call(
        _attn_kernel,
        grid=(B, nb),
        in_specs=[
            pl.BlockSpec(memory_space=pltpu.SMEM),
            pl.BlockSpec((ATT_BLOCK, ATT_WIDTH), lambda b, j: (cur(b, j), COL_AQ // ATT_WIDTH)),
            pl.BlockSpec((ATT_BLOCK, ATT_KV_WIDTH), lambda b, j: (cur(b, j), kcol)),
            pl.BlockSpec((ATT_BLOCK, ATT_KV_WIDTH), lambda b, j: (prev(b, j), kcol)),
            pl.BlockSpec((ATT_BLOCK, ATT_KV_WIDTH), lambda b, j: (cur(b, j), vcol)),
            pl.BlockSpec((ATT_BLOCK, ATT_KV_WIDTH), lambda b, j: (prev(b, j), vcol)),
            pl.BlockSpec((1, ATT_HEAD_DIM), lambda b, j: (0, 0)),
            pl.BlockSpec((1, ATT_HEAD_DIM), lambda b, j: (0, 0)),
        ],
        out_specs=pl.BlockSpec((ATT_BLOCK, ATT_WIDTH), lambda b, j: (cur(b, j), 0)),
        out_shape=jax.ShapeDtypeStruct((T, ATT_WIDTH), BF16),
        compiler_params=_params("parallel", "arbitrary"),
        name="swa_attention",
    )(sinks, proj, proj, proj, proj, proj, qg, kg)


def _shift_rows(cur, tail, j):
    cur_r = pltpu.roll(cur, j, 0)
    tail_r = pltpu.roll(tail, j, 0)
    row = lax.broadcasted_iota(jnp.int32, tail.shape, 0)
    first = jnp.where(row < j, tail_r, cur_r[:SUBLANES])
    return jnp.concatenate([first, cur_r[SUBLANES:]], axis=0)


def _conv_silu(cur, tail, w, b):
    y = cur * w[CONV_WIDTH - 1:CONV_WIDTH] + b
    for j in range(1, CONV_WIDTH):
        y = y + _shift_rows(cur, tail, j) * w[CONV_WIDTH - 1 - j:CONV_WIDTH - j]
    return y / (1.0 + jnp.exp(-y))


def _mlstm_kernel(q_ref, k_ref, v_ref, o_ref, ifc_ref, ift_ref, cw_ref, cb_ref, ng_ref,
                  out_ref, ct_ref, m_ref, qtail_ref, ktail_ref):
    L = ML_CHUNK

    @pl.when(pl.program_id(1) == 0)
    def _():
        ct_ref[...] = jnp.zeros_like(ct_ref)
        m_ref[...] = jnp.full_like(m_ref, NEG_INF)
        qtail_ref[...] = jnp.zeros_like(qtail_ref)
        ktail_ref[...] = jnp.zeros_like(ktail_ref)

    q_raw = q_ref[...].astype(F32)
    k_raw = k_ref[...].astype(F32)
    cw = cw_ref[...]
    cb = cb_ref[...]
    q_all = _conv_silu(q_raw, qtail_ref[...], cw[:, :ML_WIDTH], cb[:, :ML_WIDTH])
    k_all = _conv_silu(k_raw, ktail_ref[...], cw[:, ML_WIDTH:], cb[:, ML_WIDTH:]) * (ML_HEAD_DIM ** -0.5)
    qtail_ref[...] = q_raw[L - SUBLANES:]
    ktail_ref[...] = k_raw[L - SUBLANES:]

    def log_sigmoid(a):
        return jnp.minimum(a, 0.0) - jnp.log1p(jnp.exp(-jnp.abs(a)))

    ifc = ifc_ref[...]
    ift = ift_ref[...]
    r = lax.broadcasted_iota(jnp.int32, (L, L), 0)
    c = lax.broadcasted_iota(jnp.int32, (L, L), 1)
    causal = c <= r
    lower = causal.astype(BF16)
    upper = (r <= c).astype(BF16)
    lh, ll = _split_bf16(log_sigmoid(ifc))
    b_cols = _dot(lower, lh) + _dot(lower, ll)
    th, tl = _split_bf16(log_sigmoid(ift))
    b_rows = _dot(th, upper) + _dot(tl, upper)

    lane = lax.broadcasted_iota(jnp.int32, (L, LANES), 1)
    ones_col = (lane == 0).astype(BF16)
    ng = ng_ref[...]
    outs = []
    for h in range(ML_HEADS):
        hs = slice(h * ML_HEAD_DIM, (h + 1) * ML_HEAD_DIM)
        f = ML_HEADS + h
        i_row = ift[h:h + 1, :]
        i_col = ifc[:, h:h + 1]
        b_row = b_rows[f:f + 1, :]
        b_col = b_cols[:, f:f + 1]
        m_prev = m_ref[h, 0:1, 0:1]
        q = q_all[:, hs].astype(BF16)
        k = k_all[:, hs]
        v_aug = jnp.concatenate([v_ref[:, hs], ones_col], axis=-1)

        log_d = jnp.where(causal, b_col - b_row + i_row, NEG_INF)
        m_inter = b_col + m_prev
        m_t = jnp.maximum(jnp.max(log_d, axis=-1, keepdims=True), m_inter)
        sc = _dot_nt(q, k.astype(BF16)) * jnp.exp(log_d - m_t)
        w_inter = jnp.exp(m_inter - m_t)
        nd = _dot(sc.astype(BF16), v_aug) + w_inter * _dot(q, ct_ref[h].astype(BF16))
        num = nd[:, :ML_HEAD_DIM]
        den = nd[:, ML_HEAD_DIM:ML_HEAD_DIM + 1]
        hh = num / jnp.maximum(jnp.abs(den), jnp.exp(-m_t))

        b_last = b_col[L - 1:L, :]
        log_w = b_last - b_col + i_col
        m_new = jnp.maximum(b_last + m_prev, jnp.max(log_w, axis=0, keepdims=True))
        kw = (k * jnp.exp(log_w - m_new)).astype(BF16)
        decay = jnp.exp(b_last + m_prev - m_new)
        ct_ref[h] = decay * ct_ref[h] + _dot_tn(kw, v_aug)
        m_ref[h] = jnp.broadcast_to(m_new, m_ref.shape[1:])

        hn = _head_norm(hh, ng[:, hs])
        gate = 1.0 / (1.0 + jnp.exp(-o_ref[:, hs].astype(F32)))
        outs.append(gate * hn)
    out_ref[...] = jnp.concatenate(outs, axis=-1).astype(out_ref.dtype)


def _mlstm(proj, ifc, ift, conv_w, conv_b, ml_norm_g, B, S):
    L = ML_CHUNK
    nc = S // L
    T = B * S
    row = lambda b, c: b * nc + c
    seg = lambda col: pl.BlockSpec((L, ML_WIDTH), lambda b, c, col=col: (row(b, c), col // ML_WIDTH))
    return pl.pallas_call(
        _mlstm_kernel,
        grid=(B, nc),
        in_specs=[
            seg(COL_MQ), seg(COL_MK), seg(COL_MV), seg(COL_MO),
            pl.BlockSpec((L, LANES), lambda b, c: (row(b, c), 0)),
            pl.BlockSpec((SUBLANES, L), lambda b, c: (0, row(b, c))),
            pl.BlockSpec((CONV_WIDTH, 2 * ML_WIDTH), lambda b, c: (0, 0)),
            pl.BlockSpec((1, 2 * ML_WIDTH), lambda b, c: (0, 0)),
            pl.BlockSpec((1, ML_WIDTH), lambda b, c: (0, 0)),
        ],
        out_specs=pl.BlockSpec((L, ML_WIDTH), lambda b, c: (row(b, c), 0)),
        out_shape=jax.ShapeDtypeStruct((T, ML_WIDTH), BF16),
        scratch_shapes=[
            pltpu.VMEM((ML_HEADS, ML_HEAD_DIM, ML_AUG), F32),
            pltpu.VMEM((ML_HEADS, SUBLANES, LANES), F32),
            pltpu.VMEM((SUBLANES, ML_WIDTH), F32),
            pltpu.VMEM((SUBLANES, ML_WIDTH), F32),
        ],
        compiler_params=_params("parallel", "arbitrary"),
        name="mlstm",
    )(proj, proj, proj, proj, ifc, ift, conv_w, conv_b, ml_norm_g)


def _sigmoid(a):
    return 1.0 / (1.0 + jnp.exp(-a))


def _merge_kernel(att_ref, ml_ref, g0_ref, g1_ref, x_ref, wa_ref, wm_ref, wo_ref, gf_ref,
                  x1_ref, xnt_ref):
    a = _dot(att_ref[...], wa_ref[...])
    m = _dot(ml_ref[...], wm_ref[...])
    merged = _sigmoid(g0_ref[...].astype(F32)) * a + _sigmoid(g1_ref[...].astype(F32)) * m
    x1 = x_ref[...] + _dot(merged.astype(BF16), wo_ref[...])
    x1_ref[...] = x1
    xn = x1 * lax.rsqrt(jnp.mean(x1 * x1, axis=-1, keepdims=True) + EPS) * gf_ref[...]
    xnt_ref[...] = xn.T.astype(BF16)


def _merge(att, ml, proj, x2, wa, wm, wo, gf, tm):
    T = x2.shape[0]
    return pl.pallas_call(
        _merge_kernel,
        grid=(T // tm,),
        in_specs=[
            pl.BlockSpec((tm, ATT_WIDTH), lambda i: (i, 0)),
            pl.BlockSpec((tm, ML_WIDTH), lambda i: (i, 0)),
            pl.BlockSpec((tm, D_MODEL), lambda i: (i, 0)),
            pl.BlockSpec((tm, D_MODEL), lambda i: (i, 1)),
            pl.BlockSpec((tm, D_MODEL), lambda i: (i, 0)),
            _resident((ATT_WIDTH, D_MODEL), lambda i: (0, 0)),
            _resident((ML_WIDTH, D_MODEL), lambda i: (0, 0)),
            _resident((D_MODEL, D_MODEL), lambda i: (0, 0)),
            pl.BlockSpec((1, D_MODEL), lambda i: (0, 0)),
        ],
        out_specs=[
            pl.BlockSpec((tm, D_MODEL), lambda i: (i, 0)),
            pl.BlockSpec((D_MODEL, tm), lambda i: (0, i)),
        ],
        out_shape=[
            jax.ShapeDtypeStruct((T, D_MODEL), F32),
            jax.ShapeDtypeStruct((D_MODEL, T), BF16),
        ],
        compiler_params=_params("parallel"),
        name="merge_outproj",
    )(att, ml, proj, proj, x2, wa, wm, wo, gf)


_CAND_PAIRS = [(i, j) for i in range(PEER_TOPK) for j in range(PEER_TOPK) if (i + 1) * (j + 1) <= PEER_TOPK]
_N_CAND = len(_CAND_PAIRS)
_CAND_ROWS = -(-_N_CAND // SUBLANES) * SUBLANES


def _retrieve_kernel(xnt_ref, wq_ref, keys_ref, e1_ref, e2_ref, tau_ref,
                     sc_ref, a_ref, b_ref, cand_ref, cand2_ref):
    qpt = _dot(wq_ref[...], xnt_ref[...]).astype(BF16)
    for hc in range(2 * PEER_HEADS):
        sc_ref[hc] = _dot(keys_ref[hc], qpt[hc * PEER_KEY_DIM:(hc + 1) * PEER_KEY_DIM])

    def top_rows(e, dst_ref):
        cur = e
        for r in range(PEER_TOPK):
            m = jnp.max(cur, axis=0, keepdims=True)
            dst_ref[r:r + 1, :] = m
            cur = jnp.where(cur == m, 0.0, cur)

    def head(h, carry):
        s1 = sc_ref[2 * h]
        s2 = sc_ref[2 * h + 1]
        ex1 = jnp.exp(s1 - jnp.max(s1, axis=0, keepdims=True))
        ex2 = jnp.exp(s2 - jnp.max(s2, axis=0, keepdims=True))
        top_rows(ex1, a_ref)
        top_rows(ex2, b_ref)
        cand_ref[...] = jnp.full_like(cand_ref, -1.0)
        for n, (i, j) in enumerate(_CAND_PAIRS):
            cand_ref[n:n + 1, :] = a_ref[i:i + 1, :] * b_ref[j:j + 1, :]
        cand = cand_ref[...]
        cur = cand
        for r in range(PEER_TOPK):
            tau = jnp.max(cur, axis=0, keepdims=True)
            cur = jnp.where(cur == tau, -1.0, cur)
        sel = cand >= jnp.maximum(tau, 0.0)
        z = jnp.sum(jnp.where(sel, cand, 0.0), axis=0, keepdims=True)
        rz = 1.0 / z
        cand2_ref[...] = jnp.full_like(cand2_ref, -1.0)
        for n, (i, j) in enumerate(_CAND_PAIRS):
            cand2_ref[n:n + 1, :] = (a_ref[i:i + 1, :] * rz) * b_ref[j:j + 1, :]
        tau2 = jnp.min(jnp.where(sel, cand2_ref[...], jnp.inf), axis=0, keepdims=True)
        e1_ref[h] = ex1 * rz
        e2_ref[h] = ex2
        tau_ref[pl.ds(h, 1), :] = tau2
        return carry

    lax.fori_loop(0, PEER_HEADS, head, 0)


def _retrieve(xnt, wq_t, keys, tb):
    T = xnt.shape[1]
    qw = 2 * PEER_HEADS * PEER_KEY_DIM
    return pl.pallas_call(
        _retrieve_kernel,
        grid=(T // tb,),
        in_specs=[
            pl.BlockSpec((D_MODEL, tb), lambda i: (0, i)),
            _resident((qw, D_MODEL), lambda i: (0, 0)),
            _resident((2 * PEER_HEADS, N_KEYS, PEER_KEY_DIM), lambda i: (0, 0, 0)),
        ],
        out_specs=[
            pl.BlockSpec((PEER_HEADS, N_KEYS, tb), lambda i: (0, 0, i)),
            pl.BlockSpec((PEER_HEADS, N_KEYS, tb), lambda i: (0, 0, i)),
            pl.BlockSpec((PEER_HEADS, tb), lambda i: (0, i)),
        ],
        out_shape=[
            jax.ShapeDtypeStruct((PEER_HEADS, N_KEYS, T), F32),
            jax.ShapeDtypeStruct((PEER_HEADS, N_KEYS, T), F32),
            jax.ShapeDtypeStruct((PEER_HEADS, T), F32),
        ],
        scratch_shapes=[
            pltpu.VMEM((2 * PEER_HEADS, N_KEYS, tb), F32),
            pltpu.VMEM((PEER_TOPK, tb), F32),
            pltpu.VMEM((PEER_TOPK, tb), F32),
            pltpu.VMEM((_CAND_ROWS, tb), F32),
            pltpu.VMEM((_CAND_ROWS, tb), F32),
        ],
        compiler_params=_params("parallel"),
        name="peer_retrieve",
    )(xnt, wq_t, keys)


def _gelu(z):
    return 0.5 * z * (1.0 + lax.erf(z * (2.0 ** -0.5)))


def _experts_kernel(xnt_ref, u_ref, vt_ref, e1_ref, e2_ref, tau_ref, x1_ref, out_ref,
                    acc_ref, z_ref, w_ref, *, tsub):
    kb = pl.program_id(1)
    eb, tb = z_ref.shape

    @pl.when(kb == 0)
    def _():
        acc_ref[...] = jnp.zeros_like(acc_ref)

    z_ref[...] = _dot(u_ref[...], xnt_ref[...])
    for j in range(eb // N_KEYS):
        rs = slice(j * N_KEYS, (j + 1) * N_KEYS)
        for t in range(tb // tsub):
            ts = slice(t * tsub, (t + 1) * tsub)
            gates = jnp.zeros((N_KEYS, tsub), F32)
            for h in range(PEER_HEADS):
                p = e2_ref[h, :, ts] * e1_ref[h, j:j + 1, ts]
                gates = gates + jnp.where(p >= tau_ref[h:h + 1, ts], p, 0.0)
            w_ref[rs, ts] = (gates * _gelu(z_ref[rs, ts])).astype(BF16)
    acc_ref[...] += _dot(vt_ref[...], w_ref[...])

    @pl.when(kb == pl.num_programs(1) - 1)
    def _():
        out_ref[...] = x1_ref[...] + acc_ref[...].T


def _experts(xnt, u_bf, vt_bf, e1, e2, tau, x1, tb, eb, tsub):
    T = x1.shape[0]
    slabs = eb // N_KEYS
    return pl.pallas_call(
        functools.partial(_experts_kernel, tsub=tsub),
        grid=(T // tb, N_EXPERTS // eb),
        in_specs=[
            pl.BlockSpec((D_MODEL, tb), lambda i, k: (0, i)),
            pl.BlockSpec((eb, D_MODEL), lambda i, k: (k, 0)),
            pl.BlockSpec((D_MODEL, eb), lambda i, k: (0, k)),
            pl.BlockSpec((PEER_HEADS, slabs, tb), lambda i, k: (0, k, i)),
            pl.BlockSpec((PEER_HEADS, N_KEYS, tb), lambda i, k: (0, 0, i)),
            pl.BlockSpec((PEER_HEADS, tb), lambda i, k: (0, i)),
            pl.BlockSpec((tb, D_MODEL), lambda i, k: (i, 0)),
        ],
        out_specs=pl.BlockSpec((tb, D_MODEL), lambda i, k: (i, 0)),
        out_shape=jax.ShapeDtypeStruct((T, D_MODEL), F32),
        scratch_shapes=[
            pltpu.VMEM((D_MODEL, tb), F32),
            pltpu.VMEM((eb, tb), F32),
            pltpu.VMEM((eb, tb), BF16),
        ],
        compiler_params=_params("parallel", "arbitrary"),
        name="peer_experts",
    )(xnt, u_bf, vt_bf, e1, e2, tau, x1)


def _pick(n, pref):
    b = min(n, pref)
    while n % b:
        b //= 2
    return b


def _layer(x, norm_mix_g, w_in, b_in, conv_w, conv_b, q_norm_g, k_norm_g, sinks, ml_norm_g,
           w_proj_att, w_proj_ml, w_out, norm_ffn_g, w_peer_q, peer_keys, peer_u, peer_v):
    B, S, _ = x.shape
    T = B * S
    x2 = x.reshape(T, D_MODEL)

    segs = [(SRC_G, 2 * D_MODEL), (SRC_AQ, ATT_WIDTH), (SRC_MQ, ML_WIDTH), (SRC_MK, ML_WIDTH),
            (SRC_MV, ML_WIDTH), (SRC_MO, ML_WIDTH), (SRC_AK, ATT_KV_WIDTH), (SRC_AV, ATT_KV_WIDTH)]
    w_packed = jnp.concatenate([w_in[:, s:s + n] for s, n in segs], axis=1).astype(BF16)
    b_packed = jnp.concatenate([b_in[s:s + n] for s, n in segs])[None, :]
    n_if = 2 * ML_HEADS
    w_if = w_in[:, SRC_MI:SRC_MI + n_if]
    b_if = b_in[SRC_MI:SRC_MI + n_if]
    wif = jnp.pad(w_if, ((0, 0), (0, LANES - n_if)))
    bif = jnp.pad(b_if, (0, LANES - n_if))[None, :]
    wift = w_if.T
    bift = b_if[:, None]

    tm = _pick(T, 1024)
    proj, ifc, ift = _inproj(x2, norm_mix_g[None, :], w_packed, b_packed, wif, bif, wift, bift, tm, 512)

    att = _attention(proj, sinks, q_norm_g[None, :], k_norm_g[None, :], B, S)
    ml = _mlstm(proj, ifc, ift, conv_w, conv_b[None, :], ml_norm_g[None, :], B, S)

    x1, xnt = _merge(att, ml, proj, x2, w_proj_att.astype(BF16), w_proj_ml.astype(BF16),
                     w_out.astype(BF16), norm_ffn_g[None, :], _pick(T, 256))

    keys = peer_keys.reshape(2 * PEER_HEADS, N_KEYS, PEER_KEY_DIM).astype(BF16)
    e1, e2, tau = _retrieve(xnt, w_peer_q.T.astype(BF16), keys, _pick(T, 256))

    out = _experts(xnt, peer_u.astype(BF16), peer_v.T.astype(BF16), e1, e2, tau, x1,
                   _pick(T, 512), 1024, 256)
    return out.reshape(B, S, D_MODEL)


def kernel(x, norm_mix_g, w_in, b_in, conv_w, conv_b, q_norm_g, k_norm_g, sinks, ml_norm_g, w_proj_att, w_proj_ml, w_out, norm_ffn_g, w_peer_q, peer_keys, peer_u, peer_v):
    for l in range(norm_mix_g.shape[0]):
        x = _layer(x, norm_mix_g[l], w_in[l], b_in[l], conv_w[l], conv_b[l], q_norm_g[l], k_norm_g[l],
                   sinks[l], ml_norm_g[l], w_proj_att[l], w_proj_ml[l], w_out[l], norm_ffn_g[l],
                   w_peer_q[l], peer_keys[l], peer_u[l], peer_v[l])
    return x
```

```python
import functools

import jax
import jax.numpy as jnp
from jax import lax
from jax.experimental import pallas as pl
from jax.experimental.pallas import tpu as pltpu

F32 = jnp.float32
BF16 = jnp.bfloat16

D_MODEL = 2048
ATT_WIDTH = 1024
ATT_HEAD_DIM = 64
ATT_HEADS = 16
ATT_KV_HEADS = 4
ATT_GROUP = ATT_HEADS // ATT_KV_HEADS
ATT_KV_WIDTH = ATT_KV_HEADS * ATT_HEAD_DIM
ATT_BLOCK = 128
ML_WIDTH = 1024
ML_HEADS = 4
ML_HEAD_DIM = 256
CONV_WIDTH = 4
PEER_HEADS = 8
N_KEYS = 128
N_EXPERTS = N_KEYS * N_KEYS
PEER_KEY_DIM = 128
PEER_TOPK = 16
EPS = 1e-6
NEG_INF = -1e30

LANES = 128
SUBLANES = 8
VMEM_LIMIT = 56 * 1024 * 1024

COL_G = 0
COL_AQ = COL_G + 2 * D_MODEL
COL_MQ = COL_AQ + ATT_WIDTH
COL_MK = COL_MQ + ML_WIDTH
COL_MV = COL_MK + ML_WIDTH
COL_MO = COL_MV + ML_WIDTH
COL_AK = COL_MO + ML_WIDTH
COL_AV = COL_AK + ATT_KV_WIDTH
PACKED_WIDTH = COL_AV + ATT_KV_WIDTH
SRC_AQ = 0
SRC_AK = SRC_AQ + ATT_WIDTH
SRC_AV = SRC_AK + ATT_KV_WIDTH
SRC_MQ = SRC_AV + ATT_KV_WIDTH
SRC_MK = SRC_MQ + ML_WIDTH
SRC_MV = SRC_MK + ML_WIDTH
SRC_MO = SRC_MV + ML_WIDTH
SRC_MI = SRC_MO + ML_WIDTH
SRC_MF = SRC_MI + ML_HEADS
SRC_G = SRC_MF + ML_HEADS

ML_CHUNK = 256
ML_AUG = ML_HEAD_DIM + LANES


def _dot(a, b):
    return jnp.dot(a, b, preferred_element_type=F32)


def _dot_nt(a, b):
    return lax.dot_general(a, b, (((1,), (1,)), ((), ())), preferred_element_type=F32)


def _dot_tn(a, b):
    return lax.dot_general(a, b, (((0,), (0,)), ((), ())), preferred_element_type=F32)


def _split_bf16(a):
    hi = a.astype(BF16)
    lo = (a - hi.astype(F32)).astype(BF16)
    return hi, lo


def _params(*sem):
    return pltpu.CompilerParams(dimension_semantics=sem, vmem_limit_bytes=VMEM_LIMIT)


def _resident(shape, index_map):
    return pl.BlockSpec(shape, index_map, pipeline_mode=pl.Buffered(1))


def _inproj_kernel(x_ref, g_ref, w_ref, b_ref, wif_ref, bif_ref, o_ref, ifc_ref, ift_ref, xn_ref):
    @pl.when(pl.program_id(1) == 0)
    def _():
        x = x_ref[...]
        xn = x * lax.rsqrt(jnp.mean(x * x, axis=-1, keepdims=True) + EPS) * g_ref[...]
        xn_ref[...] = xn.astype(BF16)
        ifc = _dot(xn_ref[...], wif_ref[...]) + bif_ref[...]
        ifc_ref[...] = ifc
        ift_ref[...] = ifc.T[:SUBLANES]

    o_ref[...] = (_dot(xn_ref[...], w_ref[...]) + b_ref[...]).astype(o_ref.dtype)


def _inproj(x2, g, w_packed, b_packed, wif, bif, tm, tn):
    T = x2.shape[0]
    grid = (T // tm, PACKED_WIDTH // tn)
    return pl.pallas_call(
        _inproj_kernel,
        grid=grid,
        in_specs=[
            pl.BlockSpec((tm, D_MODEL), lambda i, j: (i, 0)),
            pl.BlockSpec((1, D_MODEL), lambda i, j: (0, 0)),
            pl.BlockSpec((D_MODEL, tn), lambda i, j: (0, j)),
            pl.BlockSpec((1, tn), lambda i, j: (0, j)),
            pl.BlockSpec((D_MODEL, LANES), lambda i, j: (0, 0)),
            pl.BlockSpec((1, LANES), lambda i, j: (0, 0)),
        ],
        out_specs=[
            pl.BlockSpec((tm, tn), lambda i, j: (i, j)),
            pl.BlockSpec((tm, LANES), lambda i, j: (i, 0)),
            pl.BlockSpec((SUBLANES, tm), lambda i, j: (0, i)),
        ],
        out_shape=[
            jax.ShapeDtypeStruct((T, PACKED_WIDTH), BF16),
            jax.ShapeDtypeStruct((T, LANES), F32),
            jax.ShapeDtypeStruct((SUBLANES, T), F32),
        ],
        scratch_shapes=[pltpu.VMEM((tm, D_MODEL), BF16)],
        compiler_params=_params("parallel", "arbitrary"),
        name="inproj",
    )(x2, g, w_packed, b_packed, wif, bif)


def _head_norm(a, g):
    return a * lax.rsqrt(jnp.mean(a * a, axis=-1, keepdims=True) + EPS) * g


def _attn_kernel(sinks_ref, q_ref, kc_ref, kp_ref, vc_ref, vp_ref, qg_ref, kg_ref, bdq_ref, bdk_ref, o_ref):
    j = pl.program_id(1)
    blk = ATT_BLOCK
    rows = ATT_GROUP * blk
    qpos = lax.broadcasted_iota(jnp.int32, (rows, 2 * blk), 0) & (blk - 1)
    kpos = lax.broadcasted_iota(jnp.int32, (rows, 2 * blk), 1)
    is_prev = kpos < blk
    no_prev = jnp.where(j > 0, 0, 2 * blk)
    valid = (jnp.where(is_prev, kpos - no_prev, qpos + blk) >= jnp.where(is_prev, qpos + 1, kpos))
    inv_dim = 1.0 / ATT_HEAD_DIM
    q = q_ref[...].astype(F32)
    q_ms = _dot((q * q).astype(BF16), bdq_ref[...]) * inv_dim
    qn = (q * lax.rsqrt(q_ms + EPS) * (qg_ref[...] * ATT_HEAD_DIM ** -0.5)).astype(BF16)
    k = jnp.concatenate([kp_ref[...], kc_ref[...]], axis=0).astype(F32)
    k_ms = _dot((k * k).astype(BF16), bdk_ref[...]) * inv_dim
    kn = (k * lax.rsqrt(k_ms + EPS) * kg_ref[...]).astype(BF16)
    v = jnp.concatenate([vp_ref[...], vc_ref[...]], axis=0)
    lane = lax.broadcasted_iota(jnp.int32, (2 * blk, ATT_HEAD_DIM), 1)
    ones_col = (lane == 0).astype(BF16)
    outs = []
    for hk in range(ATT_KV_HEADS):
        cs = slice(hk * ATT_HEAD_DIM, (hk + 1) * ATT_HEAD_DIM)
        heads = range(hk * ATT_GROUP, (hk + 1) * ATT_GROUP)
        qs = jnp.concatenate([qn[:, h * ATT_HEAD_DIM:(h + 1) * ATT_HEAD_DIM] for h in heads], axis=0)
        sink = jnp.concatenate([jnp.full((blk, 1), sinks_ref[h], F32) for h in heads], axis=0)
        s = jnp.where(valid, _dot_nt(qs, kn[:, cs]), NEG_INF)
        m = jnp.maximum(jnp.max(s, axis=-1, keepdims=True), sink)
        p = jnp.exp(s - m).astype(BF16)
        o = _dot(p, jnp.concatenate([v[:, cs], ones_col], axis=1))
        r = o[:, :ATT_HEAD_DIM] / (o[:, ATT_HEAD_DIM:ATT_HEAD_DIM + 1] + jnp.exp(sink - m))
        outs += [r[g * blk:(g + 1) * blk] for g in range(ATT_GROUP)]
    o_ref[...] = jnp.concatenate(outs, axis=-1).astype(o_ref.dtype)


def _block_ones(width, block):
    idx = jnp.arange(width) // block
    return (idx[:, None] == idx[None, :]).astype(BF16)


def _attention(proj, sinks, qg, kg, B, S):
    nb = S // ATT_BLOCK
    T = B * S
    kcol = COL_AK // ATT_KV_WIDTH
    vcol = COL_AV // ATT_KV_WIDTH
    cur = lambda b, j: b * nb + j
    prev = lambda b, j: b * nb + jnp.maximum(j - 1, 0)
    return pl.pallas_call(
        _attn_kernel,
        grid=(B, nb),
        in_specs=[
            pl.BlockSpec(memory_space=pltpu.SMEM),
            pl.BlockSpec((ATT_BLOCK, ATT_WIDTH), lambda b, j: (cur(b, j), COL_AQ // ATT_WIDTH)),
            pl.BlockSpec((ATT_BLOCK, ATT_KV_WIDTH), lambda b, j: (cur(b, j), kcol)),
            pl.BlockSpec((ATT_BLOCK, ATT_KV_WIDTH), lambda b, j: (prev(b, j), kcol)),
            pl.BlockSpec((ATT_BLOCK, ATT_KV_WIDTH), lambda b, j: (cur(b, j), vcol)),
            pl.BlockSpec((ATT_BLOCK, ATT_KV_WIDTH), lambda b, j: (prev(b, j), vcol)),
            pl.BlockSpec((1, ATT_WIDTH), lambda b, j: (0, 0)),
            pl.BlockSpec((1, ATT_KV_WIDTH), lambda b, j: (0, 0)),
            _resident((ATT_WIDTH, ATT_WIDTH), lambda b, j: (0, 0)),
            _resident((ATT_KV_WIDTH, ATT_KV_WIDTH), lambda b, j: (0, 0)),
        ],
        out_specs=pl.BlockSpec((ATT_BLOCK, ATT_WIDTH), lambda b, j: (cur(b, j), 0)),
        out_shape=jax.ShapeDtypeStruct((T, ATT_WIDTH), BF16),
        compiler_params=_params("parallel", "arbitrary"),
        name="swa_attention",
    )(sinks, proj, proj, proj, proj, proj,
      jnp.tile(qg, (1, ATT_HEADS)), jnp.tile(kg, (1, ATT_KV_HEADS)),
      _block_ones(ATT_WIDTH, ATT_HEAD_DIM), _block_ones(ATT_KV_WIDTH, ATT_HEAD_DIM))


def _shift_rows(cur, tail, j):
    cur_r = pltpu.roll(cur, j, 0)
    tail_r = pltpu.roll(tail, j, 0)
    row = lax.broadcasted_iota(jnp.int32, tail.shape, 0)
    first = jnp.where(row < j, tail_r, cur_r[:SUBLANES])
    return jnp.concatenate([first, cur_r[SUBLANES:]], axis=0)


def _conv_silu(cur, tail, w, b):
    y = cur * w[CONV_WIDTH - 1:CONV_WIDTH] + b
    for j in range(1, CONV_WIDTH):
        y = y + _shift_rows(cur, tail, j) * w[CONV_WIDTH - 1 - j:CONV_WIDTH - j]
    return y / (1.0 + jnp.exp(-y))


def _mlstm_kernel(q_ref, k_ref, v_ref, o_ref, ifc_ref, ift_ref, cw_ref, cb_ref, ng_ref,
                  out_ref, ct_ref, m_ref, qtail_ref, ktail_ref):
    L = ML_CHUNK

    @pl.when(pl.program_id(1) == 0)
    def _():
        ct_ref[...] = jnp.zeros_like(ct_ref)
        m_ref[...] = jnp.full_like(m_ref, NEG_INF)
        qtail_ref[...] = jnp.zeros_like(qtail_ref)
        ktail_ref[...] = jnp.zeros_like(ktail_ref)

    q_raw = q_ref[...].astype(F32)
    k_raw = k_ref[...].astype(F32)
    cw = cw_ref[...]
    cb = cb_ref[...]
    q_all = _conv_silu(q_raw, qtail_ref[...], cw[:, :ML_WIDTH], cb[:, :ML_WIDTH])
    k_all = _conv_silu(k_raw, ktail_ref[...], cw[:, ML_WIDTH:], cb[:, ML_WIDTH:]) * (ML_HEAD_DIM ** -0.5)
    qtail_ref[...] = q_raw[L - SUBLANES:]
    ktail_ref[...] = k_raw[L - SUBLANES:]

    def log_sigmoid(a):
        return jnp.minimum(a, 0.0) - jnp.log1p(jnp.exp(-jnp.abs(a)))

    ifc = ifc_ref[...]
    ift = ift_ref[...]
    r = lax.broadcasted_iota(jnp.int32, (L, L), 0)
    c = lax.broadcasted_iota(jnp.int32, (L, L), 1)
    causal = c <= r
    lower = causal.astype(BF16)
    upper = (r <= c).astype(BF16)
    lh, ll = _split_bf16(log_sigmoid(ifc))
    b_cols = _dot(lower, lh) + _dot(lower, ll)
    th, tl = _split_bf16(log_sigmoid(ift))
    b_rows = _dot(th, upper) + _dot(tl, upper)

    lane = lax.broadcasted_iota(jnp.int32, (L, LANES), 1)
    ones_col = (lane == 0).astype(BF16)
    ng = ng_ref[...]
    outs = []
    for h in range(ML_HEADS):
        hs = slice(h * ML_HEAD_DIM, (h + 1) * ML_HEAD_DIM)
        f = ML_HEADS + h
        i_row = ift[h:h + 1, :]
        i_col = ifc[:, h:h + 1]
        b_row = b_rows[f:f + 1, :]
        b_col = b_cols[:, f:f + 1]
        m_prev = m_ref[h, 0:1, 0:1]
        q = q_all[:, hs].astype(BF16)
        k = k_all[:, hs]
        v_aug = jnp.concatenate([v_ref[:, hs], ones_col], axis=-1)

        log_d = jnp.where(causal, b_col - b_row + i_row, NEG_INF)
        m_inter = b_col + m_prev
        m_t = jnp.maximum(jnp.max(log_d, axis=-1, keepdims=True), m_inter)
        sc = _dot_nt(q, k.astype(BF16)) * jnp.exp(log_d - m_t)
        w_inter = jnp.exp(m_inter - m_t)
        nd = _dot(sc.astype(BF16), v_aug) + w_inter * _dot(q, ct_ref[h].astype(BF16))
        num = nd[:, :ML_HEAD_DIM]
        den = nd[:, ML_HEAD_DIM:ML_HEAD_DIM + 1]
        hh = num / jnp.maximum(jnp.abs(den), jnp.exp(-m_t))

        b_last = b_col[L - 1:L, :]
        log_w = b_last - b_col + i_col
        m_new = jnp.maximum(b_last + m_prev, jnp.max(log_w, axis=0, keepdims=True))
        kw = (k * jnp.exp(log_w - m_new)).astype(BF16)
        decay = jnp.exp(b_last + m_prev - m_new)
        ct_ref[h] = decay * ct_ref[h] + _dot_tn(kw, v_aug)
        m_ref[h] = jnp.broadcast_to(m_new, m_ref.shape[1:])

        hn = _head_norm(hh, ng[:, hs])
        gate = 1.0 / (1.0 + jnp.exp(-o_ref[:, hs].astype(F32)))
        outs.append(gate * hn)
    out_ref[...] = jnp.concatenate(outs, axis=-1).astype(out_ref.dtype)


def _mlstm(proj, ifc, ift, conv_w, conv_b, ml_norm_g, B, S):
    L = ML_CHUNK
    nc = S // L
    T = B * S
    row = lambda b, c: b * nc + c
    seg = lambda col: pl.BlockSpec((L, ML_WIDTH), lambda b, c, col=col: (row(b, c), col // ML_WIDTH))
    return pl.pallas_call(
        _mlstm_kernel,
        grid=(B, nc),
        in_specs=[
            seg(COL_MQ), seg(COL_MK), seg(COL_MV), seg(COL_MO),
            pl.BlockSpec((L, LANES), lambda b, c: (row(b, c), 0)),
            pl.BlockSpec((SUBLANES, L), lambda b, c: (0, row(b, c))),
            pl.BlockSpec((CONV_WIDTH, 2 * ML_WIDTH), lambda b, c: (0, 0)),
            pl.BlockSpec((1, 2 * ML_WIDTH), lambda b, c: (0, 0)),
            pl.BlockSpec((1, ML_WIDTH), lambda b, c: (0, 0)),
        ],
        out_specs=pl.BlockSpec((L, ML_WIDTH), lambda b, c: (row(b, c), 0)),
        out_shape=jax.ShapeDtypeStruct((T, ML_WIDTH), BF16),
        scratch_shapes=[
            pltpu.VMEM((ML_HEADS, ML_HEAD_DIM, ML_AUG), F32),
            pltpu.VMEM((ML_HEADS, SUBLANES, LANES), F32),
            pltpu.VMEM((SUBLANES, ML_WIDTH), F32),
            pltpu.VMEM((SUBLANES, ML_WIDTH), F32),
        ],
        compiler_params=_params("parallel", "arbitrary"),
        name="mlstm",
    )(proj, proj, proj, proj, ifc, ift, conv_w, conv_b, ml_norm_g)


def _sigmoid(a):
    return 1.0 / (1.0 + jnp.exp(-a))


def _merge_kernel(att_ref, ml_ref, g0_ref, g1_ref, x_ref, wa_ref, wm_ref, wo_ref, gf_ref,
                  x1_ref, xnt_ref):
    a = _dot(att_ref[...], wa_ref[...])
    m = _dot(ml_ref[...], wm_ref[...])
    merged = _sigmoid(g0_ref[...].astype(F32)) * a + _sigmoid(g1_ref[...].astype(F32)) * m
    x1 = x_ref[...] + _dot(merged.astype(BF16), wo_ref[...])
    x1_ref[...] = x1
    xn = x1 * lax.rsqrt(jnp.mean(x1 * x1, axis=-1, keepdims=True) + EPS) * gf_ref[...]
    xnt_ref[...] = xn.T.astype(BF16)


def _merge(att, ml, proj, x2, wa, wm, wo, gf, tm):
    T = x2.shape[0]
    return pl.pallas_call(
        _merge_kernel,
        grid=(T // tm,),
        in_specs=[
            pl.BlockSpec((tm, ATT_WIDTH), lambda i: (i, 0)),
            pl.BlockSpec((tm, ML_WIDTH), lambda i: (i, 0)),
            pl.BlockSpec((tm, D_MODEL), lambda i: (i, 0)),
            pl.BlockSpec((tm, D_MODEL), lambda i: (i, 1)),
            pl.BlockSpec((tm, D_MODEL), lambda i: (i, 0)),
            _resident((ATT_WIDTH, D_MODEL), lambda i: (0, 0)),
            _resident((ML_WIDTH, D_MODEL), lambda i: (0, 0)),
            _resident((D_MODEL, D_MODEL), lambda i: (0, 0)),
            pl.BlockSpec((1, D_MODEL), lambda i: (0, 0)),
        ],
        out_specs=[
            pl.BlockSpec((tm, D_MODEL), lambda i: (i, 0)),
            pl.BlockSpec((D_MODEL, tm), lambda i: (0, i)),
        ],
        out_shape=[
            jax.ShapeDtypeStruct((T, D_MODEL), F32),
            jax.ShapeDtypeStruct((D_MODEL, T), BF16),
        ],
        compiler_params=_params("parallel"),
        name="merge_outproj",
    )(att, ml, proj, proj, x2, wa, wm, wo, gf)


_CAND_PAIRS = [(i, j) for i in range(PEER_TOPK) for j in range(PEER_TOPK) if (i + 1) * (j + 1) <= PEER_TOPK]
_N_CAND = len(_CAND_PAIRS)
_CAND_ROWS = -(-_N_CAND // SUBLANES) * SUBLANES


def _retrieve_kernel(xnt_ref, wq_ref, keys_ref, e1_ref, e2_ref, tau_ref,
                     sc_ref, a_ref, b_ref, cand_ref, cand2_ref):
    qpt = _dot(wq_ref[...], xnt_ref[...]).astype(BF16)
    for hc in range(2 * PEER_HEADS):
        sc_ref[hc] = _dot(keys_ref[hc], qpt[hc * PEER_KEY_DIM:(hc + 1) * PEER_KEY_DIM])

    def top_rows(e, dst_ref):
        cur = e
        for r in range(PEER_TOPK):
            m = jnp.max(cur, axis=0, keepdims=True)
            dst_ref[r:r + 1, :] = m
            cur = jnp.where(cur == m, 0.0, cur)

    def head(h, carry):
        s1 = sc_ref[2 * h]
        s2 = sc_ref[2 * h + 1]
        ex1 = jnp.exp(s1 - jnp.max(s1, axis=0, keepdims=True))
        ex2 = jnp.exp(s2 - jnp.max(s2, axis=0, keepdims=True))
        top_rows(ex1, a_ref)
        top_rows(ex2, b_ref)
        cand_ref[...] = jnp.full_like(cand_ref, -1.0)
        for n, (i, j) in enumerate(_CAND_PAIRS):
            cand_ref[n:n + 1, :] = a_ref[i:i + 1, :] * b_ref[j:j + 1, :]
        cand = cand_ref[...]
        cur = cand
        for r in range(PEER_TOPK):
            tau = jnp.max(cur, axis=0, keepdims=True)
            cur = jnp.where(cur == tau, -1.0, cur)
        sel = cand >= jnp.maximum(tau, 0.0)
        z = jnp.sum(jnp.where(sel, cand, 0.0), axis=0, keepdims=True)
        rz = 1.0 / z
        cand2_ref[...] = jnp.full_like(cand2_ref, -1.0)
        for n, (i, j) in enumerate(_CAND_PAIRS):
            cand2_ref[n:n + 1, :] = (a_ref[i:i + 1, :] * rz) * b_ref[j:j + 1, :]
        tau2 = jnp.min(jnp.where(sel, cand2_ref[...], jnp.inf), axis=0, keepdims=True)
        e1_ref[h] = ex1 * rz
        e2_ref[h] = ex2
        tau_ref[pl.ds(h, 1), :] = tau2
        return carry

    lax.fori_loop(0, PEER_HEADS, head, 0)


def _retrieve(xnt, wq_t, keys, tb):
    T = xnt.shape[1]
    qw = 2 * PEER_HEADS * PEER_KEY_DIM
    return pl.pallas_call(
        _retrieve_kernel,
        grid=(T // tb,),
        in_specs=[
            pl.BlockSpec((D_MODEL, tb), lambda i: (0, i)),
            _resident((qw, D_MODEL), lambda i: (0, 0)),
            _resident((2 * PEER_HEADS, N_KEYS, PEER_KEY_DIM), lambda i: (0, 0, 0)),
        ],
        out_specs=[
            pl.BlockSpec((PEER_HEADS, N_KEYS, tb), lambda i: (0, 0, i)),
            pl.BlockSpec((PEER_HEADS, N_KEYS, tb), lambda i: (0, 0, i)),
            pl.BlockSpec((PEER_HEADS, tb), lambda i: (0, i)),
        ],
        out_shape=[
            jax.ShapeDtypeStruct((PEER_HEADS, N_KEYS, T), F32),
            jax.ShapeDtypeStruct((PEER_HEADS, N_KEYS, T), F32),
            jax.ShapeDtypeStruct((PEER_HEADS, T), F32),
        ],
        scratch_shapes=[
            pltpu.VMEM((2 * PEER_HEADS, N_KEYS, tb), F32),
            pltpu.VMEM((PEER_TOPK, tb), F32),
            pltpu.VMEM((PEER_TOPK, tb), F32),
            pltpu.VMEM((_CAND_ROWS, tb), F32),
            pltpu.VMEM((_CAND_ROWS, tb), F32),
        ],
        compiler_params=_params("parallel"),
        name="peer_retrieve",
    )(xnt, wq_t, keys)


def _gelu(z):
    return 0.5 * z * (1.0 + lax.erf(z * (2.0 ** -0.5)))


def _experts_kernel(xnt_ref, u_ref, vt_ref, e1_ref, e2_ref, tau_ref, x1_ref, out_ref,
                    acc_ref, z_ref, w_ref, *, tsub):
    kb = pl.program_id(1)
    eb, tb = z_ref.shape

    @pl.when(kb == 0)
    def _():
        acc_ref[...] = jnp.zeros_like(acc_ref)

    z_ref[...] = _dot(u_ref[...], xnt_ref[...])
    for j in range(eb // N_KEYS):
        rs = slice(j * N_KEYS, (j + 1) * N_KEYS)
        for t in range(tb // tsub):
            ts = slice(t * tsub, (t + 1) * tsub)
            gates = jnp.zeros((N_KEYS, tsub), F32)
            for h in range(PEER_HEADS):
                p = e2_ref[h, :, ts] * e1_ref[h, j:j + 1, ts]
                gates = gates + jnp.where(p >= tau_ref[h:h + 1, ts], p, 0.0)
            w_ref[rs, ts] = (gates * _gelu(z_ref[rs, ts])).astype(BF16)
    acc_ref[...] += _dot(vt_ref[...], w_ref[...])

    @pl.when(kb == pl.num_programs(1) - 1)
    def _():
        out_ref[...] = x1_ref[...] + acc_ref[...].T


def _experts(xnt, u_bf, vt_bf, e1, e2, tau, x1, tb, eb, tsub):
    T = x1.shape[0]
    slabs = eb // N_KEYS
    return pl.pallas_call(
        functools.partial(_experts_kernel, tsub=tsub),
        grid=(T // tb, N_EXPERTS // eb),
        in_specs=[
            pl.BlockSpec((D_MODEL, tb), lambda i, k: (0, i)),
            pl.BlockSpec((eb, D_MODEL), lambda i, k: (k, 0)),
            pl.BlockSpec((D_MODEL, eb), lambda i, k: (0, k)),
            pl.BlockSpec((PEER_HEADS, slabs, tb), lambda i, k: (0, k, i)),
            pl.BlockSpec((PEER_HEADS, N_KEYS, tb), lambda i, k: (0, 0, i)),
            pl.BlockSpec((PEER_HEADS, tb), lambda i, k: (0, i)),
            pl.BlockSpec((tb, D_MODEL), lambda i, k: (i, 0)),
        ],
        out_specs=pl.BlockSpec((tb, D_MODEL), lambda i, k: (i, 0)),
        out_shape=jax.ShapeDtypeStruct((T, D_MODEL), F32),
        scratch_shapes=[
            pltpu.VMEM((D_MODEL, tb), F32),
            pltpu.VMEM((eb, tb), F32),
            pltpu.VMEM((eb, tb), BF16),
        ],
        compiler_params=_params("parallel", "arbitrary"),
        name="peer_experts",
    )(xnt, u_bf, vt_bf, e1, e2, tau, x1)


def _pick(n, pref):
    b = min(n, pref)
    while n % b:
        b //= 2
    return b


def _layer(x, norm_mix_g, w_in, b_in, conv_w, conv_b, q_norm_g, k_norm_g, sinks, ml_norm_g,
           w_proj_att, w_proj_ml, w_out, norm_ffn_g, w_peer_q, peer_keys, peer_u, peer_v):
    B, S, _ = x.shape
    T = B * S
    x2 = x.reshape(T, D_MODEL)

    segs = [(SRC_G, 2 * D_MODEL), (SRC_AQ, ATT_WIDTH), (SRC_MQ, ML_WIDTH), (SRC_MK, ML_WIDTH),
            (SRC_MV, ML_WIDTH), (SRC_MO, ML_WIDTH), (SRC_AK, ATT_KV_WIDTH), (SRC_AV, ATT_KV_WIDTH)]
    w_packed = jnp.concatenate([w_in[:, s:s + n] for s, n in segs], axis=1).astype(BF16)
    b_packed = jnp.concatenate([b_in[s:s + n] for s, n in segs])[None, :]
    n_if = 2 * ML_HEADS
    w_if = w_in[:, SRC_MI:SRC_MI + n_if]
    b_if = b_in[SRC_MI:SRC_MI + n_if]
    wif = jnp.pad(w_if, ((0, 0), (0, LANES - n_if))).astype(BF16)
    bif = jnp.pad(b_if, (0, LANES - n_if))[None, :]

    tm = _pick(T, 1024)
    proj, ifc, ift = _inproj(x2, norm_mix_g[None, :], w_packed, b_packed, wif, bif, tm, 512)

    att = _attention(proj, sinks, q_norm_g[None, :], k_norm_g[None, :], B, S)
    ml = _mlstm(proj, ifc, ift, conv_w, conv_b[None, :], ml_norm_g[None, :], B, S)

    x1, xnt = _merge(att, ml, proj, x2, w_proj_att.astype(BF16), w_proj_ml.astype(BF16),
                     w_out.astype(BF16), norm_ffn_g[None, :], _pick(T, 256))

    keys = peer_keys.reshape(2 * PEER_HEADS, N_KEYS, PEER_KEY_DIM).astype(BF16)
    e1, e2, tau = _retrieve(xnt, w_peer_q.T.astype(BF16), keys, _pick(T, 256))

    out = _experts(xnt, peer_u.astype(BF16), peer_v.T.astype(BF16), e1, e2, tau, x1,
                   _pick(T, 512), 1024, 256)
    return out.reshape(B, S, D_MODEL)


def kernel(x, norm_mix_g, w_in, b_in, conv_w, conv_b, q_norm_g, k_norm_g, sinks, ml_norm_g, w_proj_att, w_proj_ml, w_out, norm_ffn_g, w_peer_q, peer_keys, peer_u, peer_v):
    for l in range(norm_mix_g.shape[0]):
        x = _layer(x, norm_mix_g[l], w_in[l], b_in[l], conv_w[l], conv_b[l], q_norm_g[l], k_norm_g[l],
                   sinks[l], ml_norm_g[l], w_proj_att[l], w_proj_ml[l], w_out[l], norm_ffn_g[l],
                   w_peer_q[l], peer_keys[l], peer_u[l], peer_v[l])
    return x
```

```python
import functools

import jax
import jax.numpy as jnp
from jax import lax
from jax.experimental import pallas as pl
from jax.experimental.pallas import tpu as pltpu

F32 = jnp.float32
BF16 = jnp.bfloat16

D_MODEL = 2048
ATT_WIDTH = 1024
ATT_HEAD_DIM = 64
ATT_HEADS = 16
ATT_KV_HEADS = 4
ATT_GROUP = ATT_HEADS // ATT_KV_HEADS
ATT_KV_WIDTH = ATT_KV_HEADS * ATT_HEAD_DIM
ATT_BLOCK = 128
ML_WIDTH = 1024
ML_HEADS = 4
ML_HEAD_DIM = 256
CONV_WIDTH = 4
PEER_HEADS = 8
N_KEYS = 128
N_EXPERTS = N_KEYS * N_KEYS
PEER_KEY_DIM = 128
PEER_TOPK = 16
EPS = 1e-6
NEG_INF = -1e30

LANES = 128
SUBLANES = 8
VMEM_LIMIT = 56 * 1024 * 1024

COL_G = 0
COL_AQ = COL_G + 2 * D_MODEL
COL_MQ = COL_AQ + ATT_WIDTH
COL_MK = COL_MQ + ML_WIDTH
COL_MV = COL_MK + ML_WIDTH
COL_MO = COL_MV + ML_WIDTH
COL_AK = COL_MO + ML_WIDTH
COL_AV = COL_AK + ATT_KV_WIDTH
PACKED_WIDTH = COL_AV + ATT_KV_WIDTH
SRC_AQ = 0
SRC_AK = SRC_AQ + ATT_WIDTH
SRC_AV = SRC_AK + ATT_KV_WIDTH
SRC_MQ = SRC_AV + ATT_KV_WIDTH
SRC_MK = SRC_MQ + ML_WIDTH
SRC_MV = SRC_MK + ML_WIDTH
SRC_MO = SRC_MV + ML_WIDTH
SRC_MI = SRC_MO + ML_WIDTH
SRC_MF = SRC_MI + ML_HEADS
SRC_G = SRC_MF + ML_HEADS

ML_CHUNK = 256
ML_AUG = ML_HEAD_DIM + LANES


def _dot(a, b):
    return jnp.dot(a, b, preferred_element_type=F32)


def _dot_nt(a, b):
    return lax.dot_general(a, b, (((1,), (1,)), ((), ())), preferred_element_type=F32)


def _dot_tn(a, b):
    return lax.dot_general(a, b, (((0,), (0,)), ((), ())), preferred_element_type=F32)


def _split_bf16(a):
    hi = a.astype(BF16)
    lo = (a - hi.astype(F32)).astype(BF16)
    return hi, lo


def _params(*sem):
    return pltpu.CompilerParams(dimension_semantics=sem, vmem_limit_bytes=VMEM_LIMIT)


def _resident(shape, index_map):
    return pl.BlockSpec(shape, index_map, pipeline_mode=pl.Buffered(1))


def _inproj_kernel(x_ref, g_ref, w_ref, b_ref, wif_ref, bif_ref, o_ref, ifc_ref, ift_ref, xn_ref):
    @pl.when(pl.program_id(1) == 0)
    def _():
        x = x_ref[...]
        xn = x * lax.rsqrt(jnp.mean(x * x, axis=-1, keepdims=True) + EPS) * g_ref[...]
        xn_ref[...] = xn.astype(BF16)
        ifc = _dot(xn_ref[...], wif_ref[...]) + bif_ref[...]
        ifc_ref[...] = ifc
        ift_ref[...] = ifc.T[:SUBLANES]

    o_ref[...] = (_dot(xn_ref[...], w_ref[...]) + b_ref[...]).astype(o_ref.dtype)


def _inproj(x2, g, w_packed, b_packed, wif, bif, tm, tn):
    T = x2.shape[0]
    grid = (T // tm, PACKED_WIDTH // tn)
    return pl.pallas_call(
        _inproj_kernel,
        grid=grid,
        in_specs=[
            pl.BlockSpec((tm, D_MODEL), lambda i, j: (i, 0)),
            pl.BlockSpec((1, D_MODEL), lambda i, j: (0, 0)),
            pl.BlockSpec((D_MODEL, tn), lambda i, j: (0, j)),
            pl.BlockSpec((1, tn), lambda i, j: (0, j)),
            pl.BlockSpec((D_MODEL, LANES), lambda i, j: (0, 0)),
            pl.BlockSpec((1, LANES), lambda i, j: (0, 0)),
        ],
        out_specs=[
            pl.BlockSpec((tm, tn), lambda i, j: (i, j)),
            pl.BlockSpec((tm, LANES), lambda i, j: (i, 0)),
            pl.BlockSpec((SUBLANES, tm), lambda i, j: (0, i)),
        ],
        out_shape=[
            jax.ShapeDtypeStruct((T, PACKED_WIDTH), BF16),
            jax.ShapeDtypeStruct((T, LANES), F32),
            jax.ShapeDtypeStruct((SUBLANES, T), F32),
        ],
        scratch_shapes=[pltpu.VMEM((tm, D_MODEL), BF16)],
        compiler_params=_params("parallel", "arbitrary"),
        name="inproj",
    )(x2, g, w_packed, b_packed, wif, bif)


def _head_norm(a, g):
    return a * lax.rsqrt(jnp.mean(a * a, axis=-1, keepdims=True) + EPS) * g


def _attn_kernel(sinks_ref, q_ref, kc_ref, kp_ref, vc_ref, vp_ref, qg_ref, kg_ref, bdq_ref, bdk_ref, o_ref):
    j = pl.program_id(1)
    blk = ATT_BLOCK
    rows = ATT_GROUP * blk
    qpos = lax.broadcasted_iota(jnp.int32, (rows, 2 * blk), 0) & (blk - 1)
    kpos = lax.broadcasted_iota(jnp.int32, (rows, 2 * blk), 1)
    is_prev = kpos < blk
    no_prev = jnp.where(j > 0, 0, 2 * blk)
    valid = (jnp.where(is_prev, kpos - no_prev, qpos + blk) >= jnp.where(is_prev, qpos + 1, kpos))
    inv_dim = 1.0 / ATT_HEAD_DIM
    q = q_ref[...].astype(F32)
    q_ms = _dot((q * q).astype(BF16), bdq_ref[...]) * inv_dim
    qn = (q * lax.rsqrt(q_ms + EPS) * (qg_ref[...] * ATT_HEAD_DIM ** -0.5)).astype(BF16)
    k = jnp.concatenate([kp_ref[...], kc_ref[...]], axis=0).astype(F32)
    k_ms = _dot((k * k).astype(BF16), bdk_ref[...]) * inv_dim
    kn = (k * lax.rsqrt(k_ms + EPS) * kg_ref[...]).astype(BF16)
    v = jnp.concatenate([vp_ref[...], vc_ref[...]], axis=0)
    ones = jnp.ones((2 * blk, ATT_HEAD_DIM), BF16)
    scores, sinks = [], []
    for hk in range(ATT_KV_HEADS):
        cs = slice(hk * ATT_HEAD_DIM, (hk + 1) * ATT_HEAD_DIM)
        heads = range(hk * ATT_GROUP, (hk + 1) * ATT_GROUP)
        qs = jnp.concatenate([qn[:, h * ATT_HEAD_DIM:(h + 1) * ATT_HEAD_DIM] for h in heads], axis=0)
        sinks.append(jnp.concatenate([jnp.full((blk, 1), sinks_ref[h], F32) for h in heads], axis=0))
        scores.append(jnp.where(valid, _dot_nt(qs, kn[:, cs]), NEG_INF))
    maxes = [jnp.maximum(jnp.max(s, axis=-1, keepdims=True), sink) for s, sink in zip(scores, sinks)]
    probs = [jnp.exp(s - m).astype(BF16) for s, m in zip(scores, maxes)]
    outs = []
    for hk in range(ATT_KV_HEADS):
        cs = slice(hk * ATT_HEAD_DIM, (hk + 1) * ATT_HEAD_DIM)
        den = _dot(probs[hk], ones) + jnp.exp(sinks[hk] - maxes[hk])
        r = _dot(probs[hk], v[:, cs]) / den
        outs += [r[g * blk:(g + 1) * blk] for g in range(ATT_GROUP)]
    o_ref[...] = jnp.concatenate(outs, axis=-1).astype(o_ref.dtype)


def _block_ones(width, block):
    idx = jnp.arange(width) // block
    return (idx[:, None] == idx[None, :]).astype(BF16)


def _attention(proj, sinks, qg, kg, B, S):
    nb = S // ATT_BLOCK
    T = B * S
    kcol = COL_AK // ATT_KV_WIDTH
    vcol = COL_AV // ATT_KV_WIDTH
    cur = lambda b, j: b * nb + j
    prev = lambda b, j: b * nb + jnp.maximum(j - 1, 0)
    return pl.pallas_call(
        _attn_kernel,
        grid=(B, nb),
        in_specs=[
            pl.BlockSpec(memory_space=pltpu.SMEM),
            pl.BlockSpec((ATT_BLOCK, ATT_WIDTH), lambda b, j: (cur(b, j), COL_AQ // ATT_WIDTH)),
            pl.BlockSpec((ATT_BLOCK, ATT_KV_WIDTH), lambda b, j: (cur(b, j), kcol)),
            pl.BlockSpec((ATT_BLOCK, ATT_KV_WIDTH), lambda b, j: (prev(b, j), kcol)),
            pl.BlockSpec((ATT_BLOCK, ATT_KV_WIDTH), lambda b, j: (cur(b, j), vcol)),
            pl.BlockSpec((ATT_BLOCK, ATT_KV_WIDTH), lambda b, j: (prev(b, j), vcol)),
            pl.BlockSpec((1, ATT_WIDTH), lambda b, j: (0, 0)),
            pl.BlockSpec((1, ATT_KV_WIDTH), lambda b, j: (0, 0)),
            _resident((ATT_WIDTH, ATT_WIDTH), lambda b, j: (0, 0)),
            _resident((ATT_KV_WIDTH, ATT_KV_WIDTH), lambda b, j: (0, 0)),
        ],
        out_specs=pl.BlockSpec((ATT_BLOCK, ATT_WIDTH), lambda b, j: (cur(b, j), 0)),
        out_shape=jax.ShapeDtypeStruct((T, ATT_WIDTH), BF16),
        compiler_params=_params("parallel", "arbitrary"),
        name="swa_attention",
    )(sinks, proj, proj, proj, proj, proj,
      jnp.tile(qg, (1, ATT_HEADS)), jnp.tile(kg, (1, ATT_KV_HEADS)),
      _block_ones(ATT_WIDTH, ATT_HEAD_DIM), _block_ones(ATT_KV_WIDTH, ATT_HEAD_DIM))


def _shift_rows(cur, tail, j):
    cur_r = pltpu.roll(cur, j, 0)
    tail_r = pltpu.roll(tail, j, 0)
    row = lax.broadcasted_iota(jnp.int32, tail.shape, 0)
    first = jnp.where(row < j, tail_r, cur_r[:SUBLANES])
    return jnp.concatenate([first, cur_r[SUBLANES:]], axis=0)


def _conv_silu(cur, tail, w, b):
    y = cur * w[CONV_WIDTH - 1:CONV_WIDTH] + b
    for j in range(1, CONV_WIDTH):
        y = y + _shift_rows(cur, tail, j) * w[CONV_WIDTH - 1 - j:CONV_WIDTH - j]
    return y / (1.0 + jnp.exp(-y))


def _mlstm_kernel(q_ref, k_ref, v_ref, o_ref, ifc_ref, ift_ref, cw_ref, cb_ref, ng_ref,
                  out_ref, ct_ref, m_ref, qtail_ref, ktail_ref):
    L = ML_CHUNK

    @pl.when(pl.program_id(1) == 0)
    def _():
        ct_ref[...] = jnp.zeros_like(ct_ref)
        m_ref[...] = jnp.full_like(m_ref, NEG_INF)
        qtail_ref[...] = jnp.zeros_like(qtail_ref)
        ktail_ref[...] = jnp.zeros_like(ktail_ref)

    q_raw = q_ref[...].astype(F32)
    k_raw = k_ref[...].astype(F32)
    cw = cw_ref[...]
    cb = cb_ref[...]
    q_all = _conv_silu(q_raw, qtail_ref[...], cw[:, :ML_WIDTH], cb[:, :ML_WIDTH])
    k_all = _conv_silu(k_raw, ktail_ref[...], cw[:, ML_WIDTH:], cb[:, ML_WIDTH:]) * (ML_HEAD_DIM ** -0.5)
    qtail_ref[...] = q_raw[L - SUBLANES:]
    ktail_ref[...] = k_raw[L - SUBLANES:]

    def log_sigmoid(a):
        return jnp.minimum(a, 0.0) - jnp.log1p(jnp.exp(-jnp.abs(a)))

    ifc = ifc_ref[...]
    ift = ift_ref[...]
    r = lax.broadcasted_iota(jnp.int32, (L, L), 0)
    c = lax.broadcasted_iota(jnp.int32, (L, L), 1)
    causal = c <= r
    lower = causal.astype(BF16)
    upper = (r <= c).astype(BF16)
    lh, ll = _split_bf16(log_sigmoid(ifc))
    b_cols = _dot(lower, lh) + _dot(lower, ll)
    th, tl = _split_bf16(log_sigmoid(ift))
    b_rows = _dot(th, upper) + _dot(tl, upper)

    lane = lax.broadcasted_iota(jnp.int32, (L, LANES), 1)
    ones_col = (lane == 0).astype(BF16)
    ng = ng_ref[...]
    H = range(ML_HEADS)
    hs = [slice(h * ML_HEAD_DIM, (h + 1) * ML_HEAD_DIM) for h in H]
    i_row = [ift[h:h + 1, :] for h in H]
    i_col = [ifc[:, h:h + 1] for h in H]
    b_row = [b_rows[ML_HEADS + h:ML_HEADS + h + 1, :] for h in H]
    b_col = [b_cols[:, ML_HEADS + h:ML_HEADS + h + 1] for h in H]
    m_prev = [m_ref[h, 0:1, 0:1] for h in H]
    q = [q_all[:, hs[h]].astype(BF16) for h in H]
    k = [k_all[:, hs[h]] for h in H]
    v_aug = [jnp.concatenate([v_ref[:, hs[h]], ones_col], axis=-1) for h in H]

    log_d = [jnp.where(causal, b_col[h] - b_row[h] + i_row[h], NEG_INF) for h in H]
    m_inter = [b_col[h] + m_prev[h] for h in H]
    m_t = [jnp.maximum(jnp.max(log_d[h], axis=-1, keepdims=True), m_inter[h]) for h in H]
    qk = [_dot_nt(q[h], k[h].astype(BF16)) for h in H]
    inter = [_dot(q[h], ct_ref[h].astype(BF16)) for h in H]
    sc = [(qk[h] * jnp.exp(log_d[h] - m_t[h])).astype(BF16) for h in H]
    nd = [_dot(sc[h], v_aug[h]) + jnp.exp(m_inter[h] - m_t[h]) * inter[h] for h in H]
    hh = [nd[h][:, :ML_HEAD_DIM]
          / jnp.maximum(jnp.abs(nd[h][:, ML_HEAD_DIM:ML_HEAD_DIM + 1]), jnp.exp(-m_t[h])) for h in H]

    b_last = [b_col[h][L - 1:L, :] for h in H]
    log_w = [b_last[h] - b_col[h] + i_col[h] for h in H]
    m_new = [jnp.maximum(b_last[h] + m_prev[h], jnp.max(log_w[h], axis=0, keepdims=True)) for h in H]
    kw = [(k[h] * jnp.exp(log_w[h] - m_new[h])).astype(BF16) for h in H]
    for h in H:
        ct_ref[h] = jnp.exp(b_last[h] + m_prev[h] - m_new[h]) * ct_ref[h] + _dot_tn(kw[h], v_aug[h])
        m_ref[h] = jnp.broadcast_to(m_new[h], m_ref.shape[1:])

    outs = [_head_norm(hh[h], ng[:, hs[h]]) / (1.0 + jnp.exp(-o_ref[:, hs[h]].astype(F32))) for h in H]
    out_ref[...] = jnp.concatenate(outs, axis=-1).astype(out_ref.dtype)


def _mlstm(proj, ifc, ift, conv_w, conv_b, ml_norm_g, B, S):
    L = ML_CHUNK
    nc = S // L
    T = B * S
    row = lambda b, c: b * nc + c
    seg = lambda col: pl.BlockSpec((L, ML_WIDTH), lambda b, c, col=col: (row(b, c), col // ML_WIDTH))
    return pl.pallas_call(
        _mlstm_kernel,
        grid=(B, nc),
        in_specs=[
            seg(COL_MQ), seg(COL_MK), seg(COL_MV), seg(COL_MO),
            pl.BlockSpec((L, LANES), lambda b, c: (row(b, c), 0)),
            pl.BlockSpec((SUBLANES, L), lambda b, c: (0, row(b, c))),
            pl.BlockSpec((CONV_WIDTH, 2 * ML_WIDTH), lambda b, c: (0, 0)),
            pl.BlockSpec((1, 2 * ML_WIDTH), lambda b, c: (0, 0)),
            pl.BlockSpec((1, ML_WIDTH), lambda b, c: (0, 0)),
        ],
        out_specs=pl.BlockSpec((L, ML_WIDTH), lambda b, c: (row(b, c), 0)),
        out_shape=jax.ShapeDtypeStruct((T, ML_WIDTH), BF16),
        scratch_shapes=[
            pltpu.VMEM((ML_HEADS, ML_HEAD_DIM, ML_AUG), F32),
            pltpu.VMEM((ML_HEADS, SUBLANES, LANES), F32),
            pltpu.VMEM((SUBLANES, ML_WIDTH), F32),
            pltpu.VMEM((SUBLANES, ML_WIDTH), F32),
        ],
        compiler_params=_params("parallel", "arbitrary"),
        name="mlstm",
    )(proj, proj, proj, proj, ifc, ift, conv_w, conv_b, ml_norm_g)


def _sigmoid(a):
    return 1.0 / (1.0 + jnp.exp(-a))


def _merge_kernel(att_ref, ml_ref, g0_ref, g1_ref, x_ref, wa_ref, wm_ref, wo_ref, gf_ref,
                  x1_ref, xnt_ref):
    a = _dot(att_ref[...], wa_ref[...])
    m = _dot(ml_ref[...], wm_ref[...])
    merged = _sigmoid(g0_ref[...].astype(F32)) * a + _sigmoid(g1_ref[...].astype(F32)) * m
    x1 = x_ref[...] + _dot(merged.astype(BF16), wo_ref[...])
    x1_ref[...] = x1
    xn = x1 * lax.rsqrt(jnp.mean(x1 * x1, axis=-1, keepdims=True) + EPS) * gf_ref[...]
    xnt_ref[...] = xn.T.astype(BF16)


def _merge(att, ml, proj, x2, wa, wm, wo, gf, tm):
    T = x2.shape[0]
    return pl.pallas_call(
        _merge_kernel,
        grid=(T // tm,),
        in_specs=[
            pl.BlockSpec((tm, ATT_WIDTH), lambda i: (i, 0)),
            pl.BlockSpec((tm, ML_WIDTH), lambda i: (i, 0)),
            pl.BlockSpec((tm, D_MODEL), lambda i: (i, 0)),
            pl.BlockSpec((tm, D_MODEL), lambda i: (i, 1)),
            pl.BlockSpec((tm, D_MODEL), lambda i: (i, 0)),
            _resident((ATT_WIDTH, D_MODEL), lambda i: (0, 0)),
            _resident((ML_WIDTH, D_MODEL), lambda i: (0, 0)),
            _resident((D_MODEL, D_MODEL), lambda i: (0, 0)),
            pl.BlockSpec((1, D_MODEL), lambda i: (0, 0)),
        ],
        out_specs=[
            pl.BlockSpec((tm, D_MODEL), lambda i: (i, 0)),
            pl.BlockSpec((D_MODEL, tm), lambda i: (0, i)),
        ],
        out_shape=[
            jax.ShapeDtypeStruct((T, D_MODEL), F32),
            jax.ShapeDtypeStruct((D_MODEL, T), BF16),
        ],
        compiler_params=_params("parallel"),
        name="merge_outproj",
    )(att, ml, proj, proj, x2, wa, wm, wo, gf)


_CAND_PAIRS = [(i, j) for i in range(PEER_TOPK) for j in range(PEER_TOPK) if (i + 1) * (j + 1) <= PEER_TOPK]
_N_CAND = len(_CAND_PAIRS)
_CAND_ROWS = -(-_N_CAND // SUBLANES) * SUBLANES
HEADS_IN_FLIGHT = 4


def _retrieve_kernel(xnt_ref, wq_ref, keys_ref, e1_ref, e2_ref, tau_ref,
                     sc_ref, a_ref, b_ref, cand_ref, cand2_ref):
    qpt = _dot(wq_ref[...], xnt_ref[...]).astype(BF16)
    for hc in range(2 * PEER_HEADS):
        sc_ref[hc] = _dot(keys_ref[hc], qpt[hc * PEER_KEY_DIM:(hc + 1) * PEER_KEY_DIM])

    def top_rows(e, dst_ref):
        cur = e
        for r in range(PEER_TOPK):
            m = jnp.max(cur, axis=0, keepdims=True)
            dst_ref[r:r + 1, :] = m
            cur = jnp.where(cur == m, 0.0, cur)

    def head(h, a_ref, b_ref, cand_ref, cand2_ref):
        s1 = sc_ref[2 * h]
        s2 = sc_ref[2 * h + 1]
        ex1 = jnp.exp(s1 - jnp.max(s1, axis=0, keepdims=True))
        ex2 = jnp.exp(s2 - jnp.max(s2, axis=0, keepdims=True))
        top_rows(ex1, a_ref)
        top_rows(ex2, b_ref)
        cand_ref[...] = jnp.full_like(cand_ref, -1.0)
        for n, (i, j) in enumerate(_CAND_PAIRS):
            cand_ref[n:n + 1, :] = a_ref[i:i + 1, :] * b_ref[j:j + 1, :]
        cand = cand_ref[...]
        cur = cand
        for r in range(PEER_TOPK):
            tau = jnp.max(cur, axis=0, keepdims=True)
            cur = jnp.where(cur == tau, -1.0, cur)
        sel = cand >= jnp.maximum(tau, 0.0)
        z = jnp.sum(jnp.where(sel, cand, 0.0), axis=0, keepdims=True)
        rz = 1.0 / z
        cand2_ref[...] = jnp.full_like(cand2_ref, -1.0)
        for n, (i, j) in enumerate(_CAND_PAIRS):
            cand2_ref[n:n + 1, :] = (a_ref[i:i + 1, :] * rz) * b_ref[j:j + 1, :]
        tau2 = jnp.min(jnp.where(sel, cand2_ref[...], jnp.inf), axis=0, keepdims=True)
        e1_ref[h] = ex1 * rz
        e2_ref[h] = ex2
        tau_ref[pl.ds(h, 1), :] = tau2

    def head_group(g, carry):
        for slot in range(HEADS_IN_FLIGHT):
            head(g * HEADS_IN_FLIGHT + slot, a_ref.at[slot], b_ref.at[slot], cand_ref.at[slot], cand2_ref.at[slot])
        return carry

    lax.fori_loop(0, PEER_HEADS // HEADS_IN_FLIGHT, head_group, 0)


def _retrieve(xnt, wq_t, keys, tb):
    T = xnt.shape[1]
    qw = 2 * PEER_HEADS * PEER_KEY_DIM
    return pl.pallas_call(
        _retrieve_kernel,
        grid=(T // tb,),
        in_specs=[
            pl.BlockSpec((D_MODEL, tb), lambda i: (0, i)),
            _resident((qw, D_MODEL), lambda i: (0, 0)),
            _resident((2 * PEER_HEADS, N_KEYS, PEER_KEY_DIM), lambda i: (0, 0, 0)),
        ],
        out_specs=[
            pl.BlockSpec((PEER_HEADS, N_KEYS, tb), lambda i: (0, 0, i)),
            pl.BlockSpec((PEER_HEADS, N_KEYS, tb), lambda i: (0, 0, i)),
            pl.BlockSpec((PEER_HEADS, tb), lambda i: (0, i)),
        ],
        out_shape=[
            jax.ShapeDtypeStruct((PEER_HEADS, N_KEYS, T), F32),
            jax.ShapeDtypeStruct((PEER_HEADS, N_KEYS, T), F32),
            jax.ShapeDtypeStruct((PEER_HEADS, T), F32),
        ],
        scratch_shapes=[
            pltpu.VMEM((2 * PEER_HEADS, N_KEYS, tb), F32),
            pltpu.VMEM((HEADS_IN_FLIGHT, PEER_TOPK, tb), F32),
            pltpu.VMEM((HEADS_IN_FLIGHT, PEER_TOPK, tb), F32),
            pltpu.VMEM((HEADS_IN_FLIGHT, _CAND_ROWS, tb), F32),
            pltpu.VMEM((HEADS_IN_FLIGHT, _CAND_ROWS, tb), F32),
        ],
        compiler_params=_params("parallel"),
        name="peer_retrieve",
    )(xnt, wq_t, keys)


def _gelu(z):
    return 0.5 * z * (1.0 + lax.erf(z * (2.0 ** -0.5)))


def _experts_kernel(xnt_ref, u_ref, vt_ref, e1_ref, e2_ref, tau_ref, x1_ref, out_ref,
                    acc_ref, z_ref, w_ref, *, tsub):
    kb = pl.program_id(1)
    eb, tb = z_ref.shape

    @pl.when(kb == 0)
    def _():
        acc_ref[...] = jnp.zeros_like(acc_ref)

    z_ref[...] = _dot(u_ref[...], xnt_ref[...])
    for j in range(eb // N_KEYS):
        rs = slice(j * N_KEYS, (j + 1) * N_KEYS)
        for t in range(tb // tsub):
            ts = slice(t * tsub, (t + 1) * tsub)
            gates = jnp.zeros((N_KEYS, tsub), F32)
            for h in range(PEER_HEADS):
                p = e2_ref[h, :, ts] * e1_ref[h, j:j + 1, ts]
                gates = gates + jnp.where(p >= tau_ref[h:h + 1, ts], p, 0.0)
            w_ref[rs, ts] = (gates * _gelu(z_ref[rs, ts])).astype(BF16)
    acc_ref[...] += _dot(vt_ref[...], w_ref[...])

    @pl.when(kb == pl.num_programs(1) - 1)
    def _():
        out_ref[...] = x1_ref[...] + acc_ref[...].T


def _experts(xnt, u_bf, vt_bf, e1, e2, tau, x1, tb, eb, tsub):
    T = x1.shape[0]
    slabs = eb // N_KEYS
    return pl.pallas_call(
        functools.partial(_experts_kernel, tsub=tsub),
        grid=(T // tb, N_EXPERTS // eb),
        in_specs=[
            pl.BlockSpec((D_MODEL, tb), lambda i, k: (0, i)),
            pl.BlockSpec((eb, D_MODEL), lambda i, k: (k, 0)),
            pl.BlockSpec((D_MODEL, eb), lambda i, k: (0, k)),
            pl.BlockSpec((PEER_HEADS, slabs, tb), lambda i, k: (0, k, i)),
            pl.BlockSpec((PEER_HEADS, N_KEYS, tb), lambda i, k: (0, 0, i)),
            pl.BlockSpec((PEER_HEADS, tb), lambda i, k: (0, i)),
            pl.BlockSpec((tb, D_MODEL), lambda i, k: (i, 0)),
        ],
        out_specs=pl.BlockSpec((tb, D_MODEL), lambda i, k: (i, 0)),
        out_shape=jax.ShapeDtypeStruct((T, D_MODEL), F32),
        scratch_shapes=[
            pltpu.VMEM((D_MODEL, tb), F32),
            pltpu.VMEM((eb, tb), F32),
            pltpu.VMEM((eb, tb), BF16),
        ],
        compiler_params=_params("parallel", "arbitrary"),
        name="peer_experts",
    )(xnt, u_bf, vt_bf, e1, e2, tau, x1)


def _pick(n, pref):
    b = min(n, pref)
    while n % b:
        b //= 2
    return b


def _layer(x, norm_mix_g, w_in, b_in, conv_w, conv_b, q_norm_g, k_norm_g, sinks, ml_norm_g,
           w_proj_att, w_proj_ml, w_out, norm_ffn_g, w_peer_q, peer_keys, peer_u, peer_v):
    B, S, _ = x.shape
    T = B * S
    x2 = x.reshape(T, D_MODEL)

    segs = [(SRC_G, 2 * D_MODEL), (SRC_AQ, ATT_WIDTH), (SRC_MQ, ML_WIDTH), (SRC_MK, ML_WIDTH),
            (SRC_MV, ML_WIDTH), (SRC_MO, ML_WIDTH), (SRC_AK, ATT_KV_WIDTH), (SRC_AV, ATT_KV_WIDTH)]
    w_packed = jnp.concatenate([w_in[:, s:s + n] for s, n in segs], axis=1).astype(BF16)
    b_packed = jnp.concatenate([b_in[s:s + n] for s, n in segs])[None, :]
    n_if = 2 * ML_HEADS
    w_if = w_in[:, SRC_MI:SRC_MI + n_if]
    b_if = b_in[SRC_MI:SRC_MI + n_if]
    wif = jnp.pad(w_if, ((0, 0), (0, LANES - n_if))).astype(BF16)
    bif = jnp.pad(b_if, (0, LANES - n_if))[None, :]

    tm = _pick(T, 1024)
    proj, ifc, ift = _inproj(x2, norm_mix_g[None, :], w_packed, b_packed, wif, bif, tm, 512)

    att = _attention(proj, sinks, q_norm_g[None, :], k_norm_g[None, :], B, S)
    ml = _mlstm(proj, ifc, ift, conv_w, conv_b[None, :], ml_norm_g[None, :], B, S)

    x1, xnt = _merge(att, ml, proj, x2, w_proj_att.astype(BF16), w_proj_ml.astype(BF16),
                     w_out.astype(BF16), norm_ffn_g[None, :], _pick(T, 256))

    keys = peer_keys.reshape(2 * PEER_HEADS, N_KEYS, PEER_KEY_DIM).astype(BF16)
    e1, e2, tau = _retrieve(xnt, w_peer_q.T.astype(BF16), keys, _pick(T, 256))

    out = _experts(xnt, peer_u.astype(BF16), peer_v.T.astype(BF16), e1, e2, tau, x1,
                   _pick(T, 512), 1024, 256)
    return out.reshape(B, S, D_MODEL)


def kernel(x, norm_mix_g, w_in, b_in, conv_w, conv_b, q_norm_g, k_norm_g, sinks, ml_norm_g, w_proj_att, w_proj_ml, w_out, norm_ffn_g, w_peer_q, peer_keys, peer_u, peer_v):
    for l in range(norm_mix_g.shape[0]):
        x = _layer(x, norm_mix_g[l], w_in[l], b_in[l], conv_w[l], conv_b[l], q_norm_g[l], k_norm_g[l],
                   sinks[l], ml_norm_g[l], w_proj_att[l], w_proj_ml[l], w_out[l], norm_ffn_g[l],
                   w_peer_q[l], peer_keys[l], peer_u[l], peer_v[l])
    return x
```

```python
import functools

import jax
import jax.numpy as jnp
from jax import lax
from jax.experimental import pallas as pl
from jax.experimental.pallas import tpu as pltpu

F32 = jnp.float32
BF16 = jnp.bfloat16

D_MODEL = 2048
ATT_WIDTH = 1024
ATT_HEAD_DIM = 64
ATT_HEADS = 16
ATT_KV_HEADS = 4
ATT_GROUP = ATT_HEADS // ATT_KV_HEADS
ATT_KV_WIDTH = ATT_KV_HEADS * ATT_HEAD_DIM
ATT_BLOCK = 128
ML_WIDTH = 1024
ML_HEADS = 4
ML_HEAD_DIM = 256
CONV_WIDTH = 4
PEER_HEADS = 8
N_KEYS = 128
N_EXPERTS = N_KEYS * N_KEYS
PEER_KEY_DIM = 128
PEER_TOPK = 16
EPS = 1e-6
NEG_INF = -1e30

LANES = 128
SUBLANES = 8
VMEM_LIMIT = 56 * 1024 * 1024

COL_G = 0
COL_AQ = COL_G + 2 * D_MODEL
COL_MQ = COL_AQ + ATT_WIDTH
COL_MK = COL_MQ + ML_WIDTH
COL_MV = COL_MK + ML_WIDTH
COL_MO = COL_MV + ML_WIDTH
COL_AK = COL_MO + ML_WIDTH
COL_AV = COL_AK + ATT_KV_WIDTH
PACKED_WIDTH = COL_AV + ATT_KV_WIDTH
SRC_AQ = 0
SRC_AK = SRC_AQ + ATT_WIDTH
SRC_AV = SRC_AK + ATT_KV_WIDTH
SRC_MQ = SRC_AV + ATT_KV_WIDTH
SRC_MK = SRC_MQ + ML_WIDTH
SRC_MV = SRC_MK + ML_WIDTH
SRC_MO = SRC_MV + ML_WIDTH
SRC_MI = SRC_MO + ML_WIDTH
SRC_MF = SRC_MI + ML_HEADS
SRC_G = SRC_MF + ML_HEADS

ML_CHUNK = 256
ML_AUG = ML_HEAD_DIM + LANES


def _dot(a, b):
    return jnp.dot(a, b, preferred_element_type=F32)


def _dot_nt(a, b):
    return lax.dot_general(a, b, (((1,), (1,)), ((), ())), preferred_element_type=F32)


def _dot_tn(a, b):
    return lax.dot_general(a, b, (((0,), (0,)), ((), ())), preferred_element_type=F32)


def _split_bf16(a):
    hi = a.astype(BF16)
    lo = (a - hi.astype(F32)).astype(BF16)
    return hi, lo


def _params(*sem):
    return pltpu.CompilerParams(dimension_semantics=sem, vmem_limit_bytes=VMEM_LIMIT)


def _resident(shape, index_map):
    return pl.BlockSpec(shape, index_map, pipeline_mode=pl.Buffered(1))


def _inproj_kernel(x_ref, g_ref, w_ref, b_ref, wif_ref, bif_ref, o_ref, ifc_ref, ift_ref, xn_ref):
    @pl.when(pl.program_id(1) == 0)
    def _():
        x = x_ref[...]
        xn = x * lax.rsqrt(jnp.mean(x * x, axis=-1, keepdims=True) + EPS) * g_ref[...]
        xn_ref[...] = xn.astype(BF16)
        ifc = _dot(xn_ref[...], wif_ref[...]) + bif_ref[...]
        ifc_ref[...] = ifc
        ift_ref[...] = ifc.T[:SUBLANES]

    o_ref[...] = (_dot(xn_ref[...], w_ref[...]) + b_ref[...]).astype(o_ref.dtype)


def _inproj(x2, g, w_packed, b_packed, wif, bif, tm, tn):
    T = x2.shape[0]
    grid = (T // tm, PACKED_WIDTH // tn)
    return pl.pallas_call(
        _inproj_kernel,
        grid=grid,
        in_specs=[
            pl.BlockSpec((tm, D_MODEL), lambda i, j: (i, 0)),
            pl.BlockSpec((1, D_MODEL), lambda i, j: (0, 0)),
            pl.BlockSpec((D_MODEL, tn), lambda i, j: (0, j)),
            pl.BlockSpec((1, tn), lambda i, j: (0, j)),
            pl.BlockSpec((D_MODEL, LANES), lambda i, j: (0, 0)),
            pl.BlockSpec((1, LANES), lambda i, j: (0, 0)),
        ],
        out_specs=[
            pl.BlockSpec((tm, tn), lambda i, j: (i, j)),
            pl.BlockSpec((tm, LANES), lambda i, j: (i, 0)),
            pl.BlockSpec((SUBLANES, tm), lambda i, j: (0, i)),
        ],
        out_shape=[
            jax.ShapeDtypeStruct((T, PACKED_WIDTH), BF16),
            jax.ShapeDtypeStruct((T, LANES), F32),
            jax.ShapeDtypeStruct((SUBLANES, T), F32),
        ],
        scratch_shapes=[pltpu.VMEM((tm, D_MODEL), BF16)],
        compiler_params=_params("parallel", "arbitrary"),
        name="inproj",
    )(x2, g, w_packed, b_packed, wif, bif)


def _head_norm(a, g):
    return a * lax.rsqrt(jnp.mean(a * a, axis=-1, keepdims=True) + EPS) * g


def _attn_kernel(sinks_ref, q_ref, kc_ref, kp_ref, vc_ref, vp_ref, qg_ref, kg_ref, bdq_ref, bdk_ref, o_ref, *, nq):
    j = pl.program_id(1)
    blk = ATT_BLOCK
    rows = ATT_GROUP * blk
    qpos = lax.broadcasted_iota(jnp.int32, (rows, 2 * blk), 0) & (blk - 1)
    kpos = lax.broadcasted_iota(jnp.int32, (rows, 2 * blk), 1)
    is_prev = kpos < blk

    def window(no_prev):
        return jnp.where(is_prev, kpos - no_prev, qpos + blk) >= jnp.where(is_prev, qpos + 1, kpos)

    valid = [window(jnp.where(j > 0, 0, 2 * blk))] + [window(0)] * (nq - 1)
    inv_dim = 1.0 / ATT_HEAD_DIM
    q = q_ref[...].astype(F32)
    q_ms = _dot((q * q).astype(BF16), bdq_ref[...]) * inv_dim
    qn = (q * lax.rsqrt(q_ms + EPS) * (qg_ref[...] * ATT_HEAD_DIM ** -0.5)).astype(BF16)
    k = jnp.concatenate([kp_ref[...], kc_ref[...]], axis=0).astype(F32)
    k_ms = _dot((k * k).astype(BF16), bdk_ref[...]) * inv_dim
    kn = (k * lax.rsqrt(k_ms + EPS) * kg_ref[...]).astype(BF16)
    v = jnp.concatenate([vp_ref[...], vc_ref[...]], axis=0)
    ones = jnp.ones((2 * blk, ATT_HEAD_DIM), BF16)
    units = [(i, hk) for i in range(nq) for hk in range(ATT_KV_HEADS)]
    head_cols = lambda h: slice(h * ATT_HEAD_DIM, (h + 1) * ATT_HEAD_DIM)
    sinks = [jnp.concatenate([jnp.full((blk, 1), sinks_ref[h], F32)
                              for h in range(hk * ATT_GROUP, (hk + 1) * ATT_GROUP)], axis=0)
             for hk in range(ATT_KV_HEADS)]
    scores = []
    for i, hk in units:
        qs = jnp.concatenate([qn[i * blk:(i + 1) * blk, head_cols(h)]
                              for h in range(hk * ATT_GROUP, (hk + 1) * ATT_GROUP)], axis=0)
        scores.append(jnp.where(valid[i], _dot_nt(qs, kn[i * blk:(i + 2) * blk, head_cols(hk)]), NEG_INF))
    maxes = [jnp.maximum(jnp.max(s, axis=-1, keepdims=True), sinks[hk]) for s, (i, hk) in zip(scores, units)]
    probs = [jnp.exp(s - m).astype(BF16) for s, m in zip(scores, maxes)]
    outs = [[] for _ in range(nq)]
    for n, (i, hk) in enumerate(units):
        den = _dot(probs[n], ones) + jnp.exp(sinks[hk] - maxes[n])
        r = _dot(probs[n], v[i * blk:(i + 2) * blk, head_cols(hk)]) / den
        outs[i] += [r[g * blk:(g + 1) * blk] for g in range(ATT_GROUP)]
    o_ref[...] = jnp.concatenate([jnp.concatenate(o, axis=-1) for o in outs], axis=0).astype(o_ref.dtype)


def _block_ones(width, block):
    idx = jnp.arange(width) // block
    return (idx[:, None] == idx[None, :]).astype(BF16)


def _attention(proj, sinks, qg, kg, B, S):
    nb = S // ATT_BLOCK
    nq = 2 if nb % 2 == 0 else 1
    T = B * S
    kcol = COL_AK // ATT_KV_WIDTH
    vcol = COL_AV // ATT_KV_WIDTH
    steps = nb // nq
    cur = lambda b, j: b * steps + j
    prev = lambda b, j: b * nb + jnp.maximum(nq * j - 1, 0)
    return pl.pallas_call(
        functools.partial(_attn_kernel, nq=nq),
        grid=(B, steps),
        in_specs=[
            pl.BlockSpec(memory_space=pltpu.SMEM),
            pl.BlockSpec((nq * ATT_BLOCK, ATT_WIDTH), lambda b, j: (cur(b, j), COL_AQ // ATT_WIDTH)),
            pl.BlockSpec((nq * ATT_BLOCK, ATT_KV_WIDTH), lambda b, j: (cur(b, j), kcol)),
            pl.BlockSpec((ATT_BLOCK, ATT_KV_WIDTH), lambda b, j: (prev(b, j), kcol)),
            pl.BlockSpec((nq * ATT_BLOCK, ATT_KV_WIDTH), lambda b, j: (cur(b, j), vcol)),
            pl.BlockSpec((ATT_BLOCK, ATT_KV_WIDTH), lambda b, j: (prev(b, j), vcol)),
            pl.BlockSpec((1, ATT_WIDTH), lambda b, j: (0, 0)),
            pl.BlockSpec((1, ATT_KV_WIDTH), lambda b, j: (0, 0)),
            _resident((ATT_WIDTH, ATT_WIDTH), lambda b, j: (0, 0)),
            _resident((ATT_KV_WIDTH, ATT_KV_WIDTH), lambda b, j: (0, 0)),
        ],
        out_specs=pl.BlockSpec((nq * ATT_BLOCK, ATT_WIDTH), lambda b, j: (cur(b, j), 0)),
        out_shape=jax.ShapeDtypeStruct((T, ATT_WIDTH), BF16),
        compiler_params=_params("parallel", "arbitrary"),
        name="swa_attention",
    )(sinks, proj, proj, proj, proj, proj,
      jnp.tile(qg, (1, ATT_HEADS)), jnp.tile(kg, (1, ATT_KV_HEADS)),
      _block_ones(ATT_WIDTH, ATT_HEAD_DIM), _block_ones(ATT_KV_WIDTH, ATT_HEAD_DIM))


def _shift_rows(cur, tail, j):
    cur_r = pltpu.roll(cur, j, 0)
    tail_r = pltpu.roll(tail, j, 0)
    row = lax.broadcasted_iota(jnp.int32, tail.shape, 0)
    first = jnp.where(row < j, tail_r, cur_r[:SUBLANES])
    return jnp.concatenate([first, cur_r[SUBLANES:]], axis=0)


def _conv_silu(cur, tail, w, b):
    y = cur * w[CONV_WIDTH - 1:CONV_WIDTH] + b
    for j in range(1, CONV_WIDTH):
        y = y + _shift_rows(cur, tail, j) * w[CONV_WIDTH - 1 - j:CONV_WIDTH - j]
    return y / (1.0 + jnp.exp(-y))


def _mlstm_kernel(q_ref, k_ref, v_ref, o_ref, ifc_ref, ift_ref, cw_ref, cb_ref, ng_ref,
                  out_ref, ct_ref, m_ref, qtail_ref, ktail_ref):
    L = ML_CHUNK

    @pl.when(pl.program_id(1) == 0)
    def _():
        ct_ref[...] = jnp.zeros_like(ct_ref)
        m_ref[...] = jnp.full_like(m_ref, NEG_INF)
        qtail_ref[...] = jnp.zeros_like(qtail_ref)
        ktail_ref[...] = jnp.zeros_like(ktail_ref)

    q_raw = q_ref[...].astype(F32)
    k_raw = k_ref[...].astype(F32)
    cw = cw_ref[...]
    cb = cb_ref[...]
    q_all = _conv_silu(q_raw, qtail_ref[...], cw[:, :ML_WIDTH], cb[:, :ML_WIDTH])
    k_all = _conv_silu(k_raw, ktail_ref[...], cw[:, ML_WIDTH:], cb[:, ML_WIDTH:]) * (ML_HEAD_DIM ** -0.5)
    qtail_ref[...] = q_raw[L - SUBLANES:]
    ktail_ref[...] = k_raw[L - SUBLANES:]

    def log_sigmoid(a):
        return jnp.minimum(a, 0.0) - jnp.log1p(jnp.exp(-jnp.abs(a)))

    ifc = ifc_ref[...]
    ift = ift_ref[...]
    r = lax.broadcasted_iota(jnp.int32, (L, L), 0)
    c = lax.broadcasted_iota(jnp.int32, (L, L), 1)
    causal = c <= r
    lower = causal.astype(BF16)
    upper = (r <= c).astype(BF16)
    lh, ll = _split_bf16(log_sigmoid(ifc))
    b_cols = _dot(lower, lh) + _dot(lower, ll)
    th, tl = _split_bf16(log_sigmoid(ift))
    b_rows = _dot(th, upper) + _dot(tl, upper)

    lane = lax.broadcasted_iota(jnp.int32, (L, LANES), 1)
    ones_col = (lane == 0).astype(BF16)
    ng = ng_ref[...]
    H = range(ML_HEADS)
    hs = [slice(h * ML_HEAD_DIM, (h + 1) * ML_HEAD_DIM) for h in H]
    i_row = [ift[h:h + 1, :] for h in H]
    i_col = [ifc[:, h:h + 1] for h in H]
    b_row = [b_rows[ML_HEADS + h:ML_HEADS + h + 1, :] for h in H]
    b_col = [b_cols[:, ML_HEADS + h:ML_HEADS + h + 1] for h in H]
    m_prev = [m_ref[h, 0:1, 0:1] for h in H]
    q = [q_all[:, hs[h]].astype(BF16) for h in H]
    k = [k_all[:, hs[h]] for h in H]
    v_aug = [jnp.concatenate([v_ref[:, hs[h]], ones_col], axis=-1) for h in H]

    log_d = [jnp.where(causal, b_col[h] - b_row[h] + i_row[h], NEG_INF) for h in H]
    m_inter = [b_col[h] + m_prev[h] for h in H]
    m_t = [jnp.maximum(jnp.max(log_d[h], axis=-1, keepdims=True), m_inter[h]) for h in H]
    qk = [_dot_nt(q[h], k[h].astype(BF16)) for h in H]
    inter = [_dot(q[h], ct_ref[h].astype(BF16)) for h in H]
    sc = [(qk[h] * jnp.exp(log_d[h] - m_t[h])).astype(BF16) for h in H]
    nd = [_dot(sc[h], v_aug[h]) + jnp.exp(m_inter[h] - m_t[h]) * inter[h] for h in H]
    hh = [nd[h][:, :ML_HEAD_DIM]
          / jnp.maximum(jnp.abs(nd[h][:, ML_HEAD_DIM:ML_HEAD_DIM + 1]), jnp.exp(-m_t[h])) for h in H]

    b_last = [b_col[h][L - 1:L, :] for h in H]
    log_w = [b_last[h] - b_col[h] + i_col[h] for h in H]
    m_new = [jnp.maximum(b_last[h] + m_prev[h], jnp.max(log_w[h], axis=0, keepdims=True)) for h in H]
    kw = [(k[h] * jnp.exp(log_w[h] - m_new[h])).astype(BF16) for h in H]
    for h in H:
        ct_ref[h] = jnp.exp(b_last[h] + m_prev[h] - m_new[h]) * ct_ref[h] + _dot_tn(kw[h], v_aug[h])
        m_ref[h] = jnp.broadcast_to(m_new[h], m_ref.shape[1:])

    outs = [_head_norm(hh[h], ng[:, hs[h]]) / (1.0 + jnp.exp(-o_ref[:, hs[h]].astype(F32))) for h in H]
    out_ref[...] = jnp.concatenate(outs, axis=-1).astype(out_ref.dtype)


def _mlstm(proj, ifc, ift, conv_w, conv_b, ml_norm_g, B, S):
    L = ML_CHUNK
    nc = S // L
    T = B * S
    row = lambda b, c: b * nc + c
    seg = lambda col: pl.BlockSpec((L, ML_WIDTH), lambda b, c, col=col: (row(b, c), col // ML_WIDTH))
    return pl.pallas_call(
        _mlstm_kernel,
        grid=(B, nc),
        in_specs=[
            seg(COL_MQ), seg(COL_MK), seg(COL_MV), seg(COL_MO),
            pl.BlockSpec((L, LANES), lambda b, c: (row(b, c), 0)),
            pl.BlockSpec((SUBLANES, L), lambda b, c: (0, row(b, c))),
            pl.BlockSpec((CONV_WIDTH, 2 * ML_WIDTH), lambda b, c: (0, 0)),
            pl.BlockSpec((1, 2 * ML_WIDTH), lambda b, c: (0, 0)),
            pl.BlockSpec((1, ML_WIDTH), lambda b, c: (0, 0)),
        ],
        out_specs=pl.BlockSpec((L, ML_WIDTH), lambda b, c: (row(b, c), 0)),
        out_shape=jax.ShapeDtypeStruct((T, ML_WIDTH), BF16),
        scratch_shapes=[
            pltpu.VMEM((ML_HEADS, ML_HEAD_DIM, ML_AUG), F32),
            pltpu.VMEM((ML_HEADS, SUBLANES, LANES), F32),
            pltpu.VMEM((SUBLANES, ML_WIDTH), F32),
            pltpu.VMEM((SUBLANES, ML_WIDTH), F32),
        ],
        compiler_params=_params("parallel", "arbitrary"),
        name="mlstm",
    )(proj, proj, proj, proj, ifc, ift, conv_w, conv_b, ml_norm_g)


def _sigmoid(a):
    return 1.0 / (1.0 + jnp.exp(-a))


def _merge_kernel(att_ref, ml_ref, g0_ref, g1_ref, x_ref, wa_ref, wm_ref, wo_ref, gf_ref,
                  x1_ref, xnt_ref):
    a = _dot(att_ref[...], wa_ref[...])
    m = _dot(ml_ref[...], wm_ref[...])
    merged = _sigmoid(g0_ref[...].astype(F32)) * a + _sigmoid(g1_ref[...].astype(F32)) * m
    x1 = x_ref[...] + _dot(merged.astype(BF16), wo_ref[...])
    x1_ref[...] = x1
    xn = x1 * lax.rsqrt(jnp.mean(x1 * x1, axis=-1, keepdims=True) + EPS) * gf_ref[...]
    xnt_ref[...] = xn.T.astype(BF16)


def _merge(att, ml, proj, x2, wa, wm, wo, gf, tm):
    T = x2.shape[0]
    return pl.pallas_call(
        _merge_kernel,
        grid=(T // tm,),
        in_specs=[
            pl.BlockSpec((tm, ATT_WIDTH), lambda i: (i, 0)),
            pl.BlockSpec((tm, ML_WIDTH), lambda i: (i, 0)),
            pl.BlockSpec((tm, D_MODEL), lambda i: (i, 0)),
            pl.BlockSpec((tm, D_MODEL), lambda i: (i, 1)),
            pl.BlockSpec((tm, D_MODEL), lambda i: (i, 0)),
            _resident((ATT_WIDTH, D_MODEL), lambda i: (0, 0)),
            _resident((ML_WIDTH, D_MODEL), lambda i: (0, 0)),
            _resident((D_MODEL, D_MODEL), lambda i: (0, 0)),
            pl.BlockSpec((1, D_MODEL), lambda i: (0, 0)),
        ],
        out_specs=[
            pl.BlockSpec((tm, D_MODEL), lambda i: (i, 0)),
            pl.BlockSpec((D_MODEL, tm), lambda i: (0, i)),
        ],
        out_shape=[
            jax.ShapeDtypeStruct((T, D_MODEL), F32),
            jax.ShapeDtypeStruct((D_MODEL, T), BF16),
        ],
        compiler_params=_params("parallel"),
        name="merge_outproj",
    )(att, ml, proj, proj, x2, wa, wm, wo, gf)


_CAND_PAIRS = [(i, j) for i in range(PEER_TOPK) for j in range(PEER_TOPK) if (i + 1) * (j + 1) <= PEER_TOPK]
_N_CAND = len(_CAND_PAIRS)
_CAND_ROWS = -(-_N_CAND // SUBLANES) * SUBLANES
HEADS_IN_FLIGHT = 4


def _retrieve_kernel(xnt_ref, wq_ref, keys_ref, e1_ref, e2_ref, tau_ref,
                     sc_ref, a_ref, b_ref, cand_ref, cand2_ref):
    qpt = _dot(wq_ref[...], xnt_ref[...]).astype(BF16)
    for hc in range(2 * PEER_HEADS):
        sc_ref[hc] = _dot(keys_ref[hc], qpt[hc * PEER_KEY_DIM:(hc + 1) * PEER_KEY_DIM])

    def top_rows(e, dst_ref):
        cur = e
        for r in range(PEER_TOPK):
            m = jnp.max(cur, axis=0, keepdims=True)
            dst_ref[r:r + 1, :] = m
            cur = jnp.where(cur == m, 0.0, cur)

    def head(h, a_ref, b_ref, cand_ref, cand2_ref):
        s1 = sc_ref[2 * h]
        s2 = sc_ref[2 * h + 1]
        ex1 = jnp.exp(s1 - jnp.max(s1, axis=0, keepdims=True))
        ex2 = jnp.exp(s2 - jnp.max(s2, axis=0, keepdims=True))
        top_rows(ex1, a_ref)
        top_rows(ex2, b_ref)
        cand_ref[...] = jnp.full_like(cand_ref, -1.0)
        for n, (i, j) in enumerate(_CAND_PAIRS):
            cand_ref[n:n + 1, :] = a_ref[i:i + 1, :] * b_ref[j:j + 1, :]
        cand = cand_ref[...]
        cur = cand
        for r in range(PEER_TOPK):
            tau = jnp.max(cur, axis=0, keepdims=True)
            cur = jnp.where(cur == tau, -1.0, cur)
        sel = cand >= jnp.maximum(tau, 0.0)
        z = jnp.sum(jnp.where(sel, cand, 0.0), axis=0, keepdims=True)
        rz = 1.0 / z
        cand2_ref[...] = jnp.full_like(cand2_ref, -1.0)
        for n, (i, j) in enumerate(_CAND_PAIRS):
            cand2_ref[n:n + 1, :] = (a_ref[i:i + 1, :] * rz) * b_ref[j:j + 1, :]
        tau2 = jnp.min(jnp.where(sel, cand2_ref[...], jnp.inf), axis=0, keepdims=True)
        e1_ref[h] = ex1 * rz
        e2_ref[h] = ex2
        tau_ref[pl.ds(h, 1), :] = tau2

    def head_group(g, carry):
        for slot in range(HEADS_IN_FLIGHT):
            head(g * HEADS_IN_FLIGHT + slot, a_ref.at[slot], b_ref.at[slot], cand_ref.at[slot], cand2_ref.at[slot])
        return carry

    lax.fori_loop(0, PEER_HEADS // HEADS_IN_FLIGHT, head_group, 0)


def _retrieve(xnt, wq_t, keys, tb):
    T = xnt.shape[1]
    qw = 2 * PEER_HEADS * PEER_KEY_DIM
    return pl.pallas_call(
        _retrieve_kernel,
        grid=(T // tb,),
        in_specs=[
            pl.BlockSpec((D_MODEL, tb), lambda i: (0, i)),
            _resident((qw, D_MODEL), lambda i: (0, 0)),
            _resident((2 * PEER_HEADS, N_KEYS, PEER_KEY_DIM), lambda i: (0, 0, 0)),
        ],
        out_specs=[
            pl.BlockSpec((PEER_HEADS, N_KEYS, tb), lambda i: (0, 0, i)),
            pl.BlockSpec((PEER_HEADS, N_KEYS, tb), lambda i: (0, 0, i)),
            pl.BlockSpec((PEER_HEADS, tb), lambda i: (0, i)),
        ],
        out_shape=[
            jax.ShapeDtypeStruct((PEER_HEADS, N_KEYS, T), F32),
            jax.ShapeDtypeStruct((PEER_HEADS, N_KEYS, T), F32),
            jax.ShapeDtypeStruct((PEER_HEADS, T), F32),
        ],
        scratch_shapes=[
            pltpu.VMEM((2 * PEER_HEADS, N_KEYS, tb), F32),
            pltpu.VMEM((HEADS_IN_FLIGHT, PEER_TOPK, tb), F32),
            pltpu.VMEM((HEADS_IN_FLIGHT, PEER_TOPK, tb), F32),
            pltpu.VMEM((HEADS_IN_FLIGHT, _CAND_ROWS, tb), F32),
            pltpu.VMEM((HEADS_IN_FLIGHT, _CAND_ROWS, tb), F32),
        ],
        compiler_params=_params("parallel"),
        name="peer_retrieve",
    )(xnt, wq_t, keys)


def _gelu(z):
    return 0.5 * z * (1.0 + lax.erf(z * (2.0 ** -0.5)))


def _experts_kernel(xnt_ref, u_ref, vt_ref, e1_ref, e2_ref, tau_ref, x1_ref, out_ref,
                    acc_ref, z_ref, w_ref, *, tsub):
    kb = pl.program_id(1)
    eb, tb = z_ref.shape

    @pl.when(kb == 0)
    def _():
        acc_ref[...] = jnp.zeros_like(acc_ref)

    z_ref[...] = _dot(u_ref[...], xnt_ref[...])
    for j in range(eb // N_KEYS):
        rs = slice(j * N_KEYS, (j + 1) * N_KEYS)
        for t in range(tb // tsub):
            ts = slice(t * tsub, (t + 1) * tsub)
            gates = jnp.zeros((N_KEYS, tsub), F32)
            for h in range(PEER_HEADS):
                p = e2_ref[h, :, ts] * e1_ref[h, j:j + 1, ts]
                gates = gates + jnp.where(p >= tau_ref[h:h + 1, ts], p, 0.0)
            w_ref[rs, ts] = (gates * _gelu(z_ref[rs, ts])).astype(BF16)
    acc_ref[...] += _dot(vt_ref[...], w_ref[...])

    @pl.when(kb == pl.num_programs(1) - 1)
    def _():
        out_ref[...] = x1_ref[...] + acc_ref[...].T


def _experts(xnt, u_bf, vt_bf, e1, e2, tau, x1, tb, eb, tsub):
    T = x1.shape[0]
    slabs = eb // N_KEYS
    return pl.pallas_call(
        functools.partial(_experts_kernel, tsub=tsub),
        grid=(T // tb, N_EXPERTS // eb),
        in_specs=[
            pl.BlockSpec((D_MODEL, tb), lambda i, k: (0, i)),
            pl.BlockSpec((eb, D_MODEL), lambda i, k: (k, 0)),
            pl.BlockSpec((D_MODEL, eb), lambda i, k: (0, k)),
            pl.BlockSpec((PEER_HEADS, slabs, tb), lambda i, k: (0, k, i)),
            pl.BlockSpec((PEER_HEADS, N_KEYS, tb), lambda i, k: (0, 0, i)),
            pl.BlockSpec((PEER_HEADS, tb), lambda i, k: (0, i)),
            pl.BlockSpec((tb, D_MODEL), lambda i, k: (i, 0)),
        ],
        out_specs=pl.BlockSpec((tb, D_MODEL), lambda i, k: (i, 0)),
        out_shape=jax.ShapeDtypeStruct((T, D_MODEL), F32),
        scratch_shapes=[
            pltpu.VMEM((D_MODEL, tb), F32),
            pltpu.VMEM((eb, tb), F32),
            pltpu.VMEM((eb, tb), BF16),
        ],
        compiler_params=_params("parallel", "arbitrary"),
        name="peer_experts",
    )(xnt, u_bf, vt_bf, e1, e2, tau, x1)


def _pick(n, pref):
    b = min(n, pref)
    while n % b:
        b //= 2
    return b


def _layer(x, norm_mix_g, w_in, b_in, conv_w, conv_b, q_norm_g, k_norm_g, sinks, ml_norm_g,
           w_proj_att, w_proj_ml, w_out, norm_ffn_g, w_peer_q, peer_keys, peer_u, peer_v):
    B, S, _ = x.shape
    T = B * S
    x2 = x.reshape(T, D_MODEL)

    segs = [(SRC_G, 2 * D_MODEL), (SRC_AQ, ATT_WIDTH), (SRC_MQ, ML_WIDTH), (SRC_MK, ML_WIDTH),
            (SRC_MV, ML_WIDTH), (SRC_MO, ML_WIDTH), (SRC_AK, ATT_KV_WIDTH), (SRC_AV, ATT_KV_WIDTH)]
    w_packed = jnp.concatenate([w_in[:, s:s + n] for s, n in segs], axis=1).astype(BF16)
    b_packed = jnp.concatenate([b_in[s:s + n] for s, n in segs])[None, :]
    n_if = 2 * ML_HEADS
    w_if = w_in[:, SRC_MI:SRC_MI + n_if]
    b_if = b_in[SRC_MI:SRC_MI + n_if]
    wif = jnp.pad(w_if, ((0, 0), (0, LANES - n_if))).astype(BF16)
    bif = jnp.pad(b_if, (0, LANES - n_if))[None, :]

    tm = _pick(T, 1024)
    proj, ifc, ift = _inproj(x2, norm_mix_g[None, :], w_packed, b_packed, wif, bif, tm, 512)

    att = _attention(proj, sinks, q_norm_g[None, :], k_norm_g[None, :], B, S)
    ml = _mlstm(proj, ifc, ift, conv_w, conv_b[None, :], ml_norm_g[None, :], B, S)

    x1, xnt = _merge(att, ml, proj, x2, w_proj_att.astype(BF16), w_proj_ml.astype(BF16),
                     w_out.astype(BF16), norm_ffn_g[None, :], _pick(T, 512))

    keys = peer_keys.reshape(2 * PEER_HEADS, N_KEYS, PEER_KEY_DIM).astype(BF16)
    e1, e2, tau = _retrieve(xnt, w_peer_q.T.astype(BF16), keys, _pick(T, 512))

    out = _experts(xnt, peer_u.astype(BF16), peer_v.T.astype(BF16), e1, e2, tau, x1,
                   _pick(T, 512), 1024, 256)
    return out.reshape(B, S, D_MODEL)


def kernel(x, norm_mix_g, w_in, b_in, conv_w, conv_b, q_norm_g, k_norm_g, sinks, ml_norm_g, w_proj_att, w_proj_ml, w_out, norm_ffn_g, w_peer_q, peer_keys, peer_u, peer_v):
    for l in range(norm_mix_g.shape[0]):
        x = _layer(x, norm_mix_g[l], w_in[l], b_in[l], conv_w[l], conv_b[l], q_norm_g[l], k_norm_g[l],
                   sinks[l], ml_norm_g[l], w_proj_att[l], w_proj_ml[l], w_out[l], norm_ffn_g[l],
                   w_peer_q[l], peer_keys[l], peer_u[l], peer_v[l])
    return x
```

```python
import functools

import jax
import jax.numpy as jnp
from jax import lax
from jax.experimental import pallas as pl
from jax.experimental.pallas import tpu as pltpu

F32 = jnp.float32
BF16 = jnp.bfloat16

D_MODEL = 2048
ATT_WIDTH = 1024
ATT_HEAD_DIM = 64
ATT_HEADS = 16
ATT_KV_HEADS = 4
ATT_GROUP = ATT_HEADS // ATT_KV_HEADS
ATT_KV_WIDTH = ATT_KV_HEADS * ATT_HEAD_DIM
ATT_BLOCK = 128
ML_WIDTH = 1024
ML_HEADS = 4
ML_HEAD_DIM = 256
CONV_WIDTH = 4
PEER_HEADS = 8
N_KEYS = 128
N_EXPERTS = N_KEYS * N_KEYS
PEER_KEY_DIM = 128
PEER_TOPK = 16
EPS = 1e-6
NEG_INF = -1e30

LANES = 128
SUBLANES = 8
VMEM_LIMIT = 56 * 1024 * 1024

COL_G = 0
COL_AQ = COL_G + 2 * D_MODEL
COL_MQ = COL_AQ + ATT_WIDTH
COL_MK = COL_MQ + ML_WIDTH
COL_MV = COL_MK + ML_WIDTH
COL_MO = COL_MV + ML_WIDTH
COL_AK = COL_MO + ML_WIDTH
COL_AV = COL_AK + ATT_KV_WIDTH
PACKED_WIDTH = COL_AV + ATT_KV_WIDTH
SRC_AQ = 0
SRC_AK = SRC_AQ + ATT_WIDTH
SRC_AV = SRC_AK + ATT_KV_WIDTH
SRC_MQ = SRC_AV + ATT_KV_WIDTH
SRC_MK = SRC_MQ + ML_WIDTH
SRC_MV = SRC_MK + ML_WIDTH
SRC_MO = SRC_MV + ML_WIDTH
SRC_MI = SRC_MO + ML_WIDTH
SRC_MF = SRC_MI + ML_HEADS
SRC_G = SRC_MF + ML_HEADS

ML_CHUNK = 256
ML_AUG = ML_HEAD_DIM + LANES


def _dot(a, b):
    return jnp.dot(a, b, preferred_element_type=F32)


def _dot_nt(a, b):
    return lax.dot_general(a, b, (((1,), (1,)), ((), ())), preferred_element_type=F32)


def _dot_tn(a, b):
    return lax.dot_general(a, b, (((0,), (0,)), ((), ())), preferred_element_type=F32)


def _split_bf16(a):
    hi = a.astype(BF16)
    lo = (a - hi.astype(F32)).astype(BF16)
    return hi, lo


def _params(*sem):
    return pltpu.CompilerParams(dimension_semantics=sem, vmem_limit_bytes=VMEM_LIMIT)


def _resident(shape, index_map):
    return pl.BlockSpec(shape, index_map, pipeline_mode=pl.Buffered(1))


def _inproj_kernel(x_ref, g_ref, w_ref, b_ref, wif_ref, bif_ref, o_ref, ifc_ref, ift_ref, xn_ref):
    @pl.when(pl.program_id(1) == 0)
    def _():
        x = x_ref[...]
        xn = x * lax.rsqrt(jnp.mean(x * x, axis=-1, keepdims=True) + EPS) * g_ref[...]
        xn_ref[...] = xn.astype(BF16)
        ifc = _dot(xn_ref[...], wif_ref[...]) + bif_ref[...]
        ifc_ref[...] = ifc
        ift_ref[...] = ifc.T[:SUBLANES]

    o_ref[...] = (_dot(xn_ref[...], w_ref[...]) + b_ref[...]).astype(o_ref.dtype)


def _inproj(x2, g, w_packed, b_packed, wif, bif, tm, tn):
    T = x2.shape[0]
    grid = (T // tm, PACKED_WIDTH // tn)
    return pl.pallas_call(
        _inproj_kernel,
        grid=grid,
        in_specs=[
            pl.BlockSpec((tm, D_MODEL), lambda i, j: (i, 0)),
            pl.BlockSpec((1, D_MODEL), lambda i, j: (0, 0)),
            pl.BlockSpec((D_MODEL, tn), lambda i, j: (0, j)),
            pl.BlockSpec((1, tn), lambda i, j: (0, j)),
            pl.BlockSpec((D_MODEL, LANES), lambda i, j: (0, 0)),
            pl.BlockSpec((1, LANES), lambda i, j: (0, 0)),
        ],
        out_specs=[
            pl.BlockSpec((tm, tn), lambda i, j: (i, j)),
            pl.BlockSpec((tm, LANES), lambda i, j: (i, 0)),
            pl.BlockSpec((SUBLANES, tm), lambda i, j: (0, i)),
        ],
        out_shape=[
            jax.ShapeDtypeStruct((T, PACKED_WIDTH), BF16),
            jax.ShapeDtypeStruct((T, LANES), F32),
            jax.ShapeDtypeStruct((SUBLANES, T), F32),
        ],
        scratch_shapes=[pltpu.VMEM((tm, D_MODEL), BF16)],
        compiler_params=_params("parallel", "arbitrary"),
        name="inproj",
    )(x2, g, w_packed, b_packed, wif, bif)


def _head_norm(a, g):
    return a * lax.rsqrt(jnp.mean(a * a, axis=-1, keepdims=True) + EPS) * g


def _attn_kernel(sinks_ref, q_ref, kc_ref, kp_ref, vc_ref, vp_ref, qg_ref, kg_ref, bdq_ref, bdk_ref, o_ref, *, nq):
    j = pl.program_id(1)
    blk = ATT_BLOCK
    rows = ATT_GROUP * blk
    qpos = lax.broadcasted_iota(jnp.int32, (rows, 2 * blk), 0) & (blk - 1)
    kpos = lax.broadcasted_iota(jnp.int32, (rows, 2 * blk), 1)
    is_prev = kpos < blk

    def window(no_prev):
        return jnp.where(is_prev, kpos - no_prev, qpos + blk) >= jnp.where(is_prev, qpos + 1, kpos)

    valid = [window(jnp.where(j > 0, 0, 2 * blk))] + [window(0)] * (nq - 1)
    inv_dim = 1.0 / ATT_HEAD_DIM
    q = q_ref[...].astype(F32)
    q_ms = _dot((q * q).astype(BF16), bdq_ref[...]) * inv_dim
    qn = (q * lax.rsqrt(q_ms + EPS) * (qg_ref[...] * ATT_HEAD_DIM ** -0.5)).astype(BF16)
    k = jnp.concatenate([kp_ref[...], kc_ref[...]], axis=0).astype(F32)
    k_ms = _dot((k * k).astype(BF16), bdk_ref[...]) * inv_dim
    kn = (k * lax.rsqrt(k_ms + EPS) * kg_ref[...]).astype(BF16)
    v = jnp.concatenate([vp_ref[...], vc_ref[...]], axis=0)
    ones = jnp.ones((2 * blk, ATT_HEAD_DIM), BF16)
    units = [(i, hk) for i in range(nq) for hk in range(ATT_KV_HEADS)]
    head_cols = lambda h: slice(h * ATT_HEAD_DIM, (h + 1) * ATT_HEAD_DIM)
    sinks = [jnp.concatenate([jnp.full((blk, 1), sinks_ref[h], F32)
                              for h in range(hk * ATT_GROUP, (hk + 1) * ATT_GROUP)], axis=0)
             for hk in range(ATT_KV_HEADS)]
    scores = []
    for i, hk in units:
        qs = jnp.concatenate([qn[i * blk:(i + 1) * blk, head_cols(h)]
                              for h in range(hk * ATT_GROUP, (hk + 1) * ATT_GROUP)], axis=0)
        scores.append(jnp.where(valid[i], _dot_nt(qs, kn[i * blk:(i + 2) * blk, head_cols(hk)]), NEG_INF))
    maxes = [jnp.maximum(jnp.max(s, axis=-1, keepdims=True), sinks[hk]) for s, (i, hk) in zip(scores, units)]
    probs = [jnp.exp(s - m).astype(BF16) for s, m in zip(scores, maxes)]
    outs = [[] for _ in range(nq)]
    for n, (i, hk) in enumerate(units):
        den = _dot(probs[n], ones) + jnp.exp(sinks[hk] - maxes[n])
        r = _dot(probs[n], v[i * blk:(i + 2) * blk, head_cols(hk)]) / den
        outs[i] += [r[g * blk:(g + 1) * blk] for g in range(ATT_GROUP)]
    o_ref[...] = jnp.concatenate([jnp.concatenate(o, axis=-1) for o in outs], axis=0).astype(o_ref.dtype)


def _block_ones(width, block):
    idx = jnp.arange(width) // block
    return (idx[:, None] == idx[None, :]).astype(BF16)


def _attention(proj, sinks, qg, kg, B, S):
    nb = S // ATT_BLOCK
    nq = 2 if nb % 2 == 0 else 1
    T = B * S
    kcol = COL_AK // ATT_KV_WIDTH
    vcol = COL_AV // ATT_KV_WIDTH
    steps = nb // nq
    cur = lambda b, j: b * steps + j
    prev = lambda b, j: b * nb + jnp.maximum(nq * j - 1, 0)
    return pl.pallas_call(
        functools.partial(_attn_kernel, nq=nq),
        grid=(B, steps),
        in_specs=[
            pl.BlockSpec(memory_space=pltpu.SMEM),
            pl.BlockSpec((nq * ATT_BLOCK, ATT_WIDTH), lambda b, j: (cur(b, j), COL_AQ // ATT_WIDTH)),
            pl.BlockSpec((nq * ATT_BLOCK, ATT_KV_WIDTH), lambda b, j: (cur(b, j), kcol)),
            pl.BlockSpec((ATT_BLOCK, ATT_KV_WIDTH), lambda b, j: (prev(b, j), kcol)),
            pl.BlockSpec((nq * ATT_BLOCK, ATT_KV_WIDTH), lambda b, j: (cur(b, j), vcol)),
            pl.BlockSpec((ATT_BLOCK, ATT_KV_WIDTH), lambda b, j: (prev(b, j), vcol)),
            pl.BlockSpec((1, ATT_WIDTH), lambda b, j: (0, 0)),
            pl.BlockSpec((1, ATT_KV_WIDTH), lambda b, j: (0, 0)),
            _resident((ATT_WIDTH, ATT_WIDTH), lambda b, j: (0, 0)),
            _resident((ATT_KV_WIDTH, ATT_KV_WIDTH), lambda b, j: (0, 0)),
        ],
        out_specs=pl.BlockSpec((nq * ATT_BLOCK, ATT_WIDTH), lambda b, j: (cur(b, j), 0)),
        out_shape=jax.ShapeDtypeStruct((T, ATT_WIDTH), BF16),
        compiler_params=_params("parallel", "arbitrary"),
        name="swa_attention",
    )(sinks, proj, proj, proj, proj, proj,
      jnp.tile(qg, (1, ATT_HEADS)), jnp.tile(kg, (1, ATT_KV_HEADS)),
      _block_ones(ATT_WIDTH, ATT_HEAD_DIM), _block_ones(ATT_KV_WIDTH, ATT_HEAD_DIM))


def _shift_rows(cur, tail, j):
    cur_r = pltpu.roll(cur, j, 0)
    tail_r = pltpu.roll(tail, j, 0)
    row = lax.broadcasted_iota(jnp.int32, tail.shape, 0)
    first = jnp.where(row < j, tail_r, cur_r[:SUBLANES])
    return jnp.concatenate([first, cur_r[SUBLANES:]], axis=0)


def _conv_silu(cur, tail, w, b):
    y = cur * w[CONV_WIDTH - 1:CONV_WIDTH] + b
    for j in range(1, CONV_WIDTH):
        y = y + _shift_rows(cur, tail, j) * w[CONV_WIDTH - 1 - j:CONV_WIDTH - j]
    return y / (1.0 + jnp.exp(-y))


def _mlstm_kernel(q_ref, k_ref, v_ref, o_ref, ifc_ref, ift_ref, cw_ref, cb_ref, ng_ref,
                  out_ref, ct_ref, m_ref, qtail_ref, ktail_ref):
    L = ML_CHUNK

    @pl.when(pl.program_id(1) == 0)
    def _():
        ct_ref[...] = jnp.zeros_like(ct_ref)
        m_ref[...] = jnp.full_like(m_ref, NEG_INF)
        qtail_ref[...] = jnp.zeros_like(qtail_ref)
        ktail_ref[...] = jnp.zeros_like(ktail_ref)

    q_raw = q_ref[...].astype(F32)
    k_raw = k_ref[...].astype(F32)
    cw = cw_ref[...]
    cb = cb_ref[...]
    q_all = _conv_silu(q_raw, qtail_ref[...], cw[:, :ML_WIDTH], cb[:, :ML_WIDTH])
    k_all = _conv_silu(k_raw, ktail_ref[...], cw[:, ML_WIDTH:], cb[:, ML_WIDTH:]) * (ML_HEAD_DIM ** -0.5)
    qtail_ref[...] = q_raw[L - SUBLANES:]
    ktail_ref[...] = k_raw[L - SUBLANES:]

    def log_sigmoid(a):
        return jnp.minimum(a, 0.0) - jnp.log1p(jnp.exp(-jnp.abs(a)))

    ifc = ifc_ref[...]
    ift = ift_ref[...]
    r = lax.broadcasted_iota(jnp.int32, (L, L), 0)
    c = lax.broadcasted_iota(jnp.int32, (L, L), 1)
    causal = c <= r
    lower = causal.astype(BF16)
    upper = (r <= c).astype(BF16)
    lh, ll = _split_bf16(log_sigmoid(ifc))
    b_cols = _dot(lower, lh) + _dot(lower, ll)
    th, tl = _split_bf16(log_sigmoid(ift))
    b_rows = _dot(th, upper) + _dot(tl, upper)

    lane = lax.broadcasted_iota(jnp.int32, (L, LANES), 1)
    ones_col = (lane == 0).astype(BF16)
    ng = ng_ref[...]
    H = range(ML_HEADS)
    hs = [slice(h * ML_HEAD_DIM, (h + 1) * ML_HEAD_DIM) for h in H]
    i_row = [ift[h:h + 1, :] for h in H]
    i_col = [ifc[:, h:h + 1] for h in H]
    b_row = [b_rows[ML_HEADS + h:ML_HEADS + h + 1, :] for h in H]
    b_col = [b_cols[:, ML_HEADS + h:ML_HEADS + h + 1] for h in H]
    m_prev = [m_ref[h, 0:1, 0:1] for h in H]
    q = [q_all[:, hs[h]].astype(BF16) for h in H]
    k = [k_all[:, hs[h]] for h in H]
    v_aug = [jnp.concatenate([v_ref[:, hs[h]], ones_col], axis=-1) for h in H]

    log_d = [jnp.where(causal, b_col[h] - b_row[h] + i_row[h], NEG_INF) for h in H]
    m_inter = [b_col[h] + m_prev[h] for h in H]
    m_t = [jnp.maximum(jnp.max(log_d[h], axis=-1, keepdims=True), m_inter[h]) for h in H]
    qk = [_dot_nt(q[h], k[h].astype(BF16)) for h in H]
    inter = [_dot(q[h], ct_ref[h].astype(BF16)) for h in H]
    sc = [(qk[h] * jnp.exp(log_d[h] - m_t[h])).astype(BF16) for h in H]
    nd = [_dot(sc[h], v_aug[h]) + jnp.exp(m_inter[h] - m_t[h]) * inter[h] for h in H]
    hh = [nd[h][:, :ML_HEAD_DIM]
          / jnp.maximum(jnp.abs(nd[h][:, ML_HEAD_DIM:ML_HEAD_DIM + 1]), jnp.exp(-m_t[h])) for h in H]

    b_last = [b_col[h][L - 1:L, :] for h in H]
    log_w = [b_last[h] - b_col[h] + i_col[h] for h in H]
    m_new = [jnp.maximum(b_last[h] + m_prev[h], jnp.max(log_w[h], axis=0, keepdims=True)) for h in H]
    kw = [(k[h] * jnp.exp(log_w[h] - m_new[h])).astype(BF16) for h in H]
    for h in H:
        ct_ref[h] = jnp.exp(b_last[h] + m_prev[h] - m_new[h]) * ct_ref[h] + _dot_tn(kw[h], v_aug[h])
        m_ref[h] = jnp.broadcast_to(m_new[h], m_ref.shape[1:])

    outs = [_head_norm(hh[h], ng[:, hs[h]]) / (1.0 + jnp.exp(-o_ref[:, hs[h]].astype(F32))) for h in H]
    out_ref[...] = jnp.concatenate(outs, axis=-1).astype(out_ref.dtype)


def _mlstm(proj, ifc, ift, conv_w, conv_b, ml_norm_g, B, S):
    L = ML_CHUNK
    nc = S // L
    T = B * S
    row = lambda b, c: b * nc + c
    seg = lambda col: pl.BlockSpec((L, ML_WIDTH), lambda b, c, col=col: (row(b, c), col // ML_WIDTH))
    return pl.pallas_call(
        _mlstm_kernel,
        grid=(B, nc),
        in_specs=[
            seg(COL_MQ), seg(COL_MK), seg(COL_MV), seg(COL_MO),
            pl.BlockSpec((L, LANES), lambda b, c: (row(b, c), 0)),
            pl.BlockSpec((SUBLANES, L), lambda b, c: (0, row(b, c))),
            pl.BlockSpec((CONV_WIDTH, 2 * ML_WIDTH), lambda b, c: (0, 0)),
            pl.BlockSpec((1, 2 * ML_WIDTH), lambda b, c: (0, 0)),
            pl.BlockSpec((1, ML_WIDTH), lambda b, c: (0, 0)),
        ],
        out_specs=pl.BlockSpec((L, ML_WIDTH), lambda b, c: (row(b, c), 0)),
        out_shape=jax.ShapeDtypeStruct((T, ML_WIDTH), BF16),
        scratch_shapes=[
            pltpu.VMEM((ML_HEADS, ML_HEAD_DIM, ML_AUG), F32),
            pltpu.VMEM((ML_HEADS, SUBLANES, LANES), F32),
            pltpu.VMEM((SUBLANES, ML_WIDTH), F32),
            pltpu.VMEM((SUBLANES, ML_WIDTH), F32),
        ],
        compiler_params=_params("parallel", "arbitrary"),
        name="mlstm",
    )(proj, proj, proj, proj, ifc, ift, conv_w, conv_b, ml_norm_g)


def _sigmoid(a):
    return 1.0 / (1.0 + jnp.exp(-a))


def _merge_kernel(att_ref, ml_ref, g0_ref, g1_ref, x_ref, wa_ref, wm_ref, wo_ref, gf_ref,
                  x1_ref, xnt_ref):
    a = _dot(att_ref[...], wa_ref[...])
    m = _dot(ml_ref[...], wm_ref[...])
    merged = _sigmoid(g0_ref[...].astype(F32)) * a + _sigmoid(g1_ref[...].astype(F32)) * m
    x1 = x_ref[...] + _dot(merged.astype(BF16), wo_ref[...])
    x1_ref[...] = x1
    xn = x1 * lax.rsqrt(jnp.mean(x1 * x1, axis=-1, keepdims=True) + EPS) * gf_ref[...]
    xnt_ref[...] = xn.T.astype(BF16)


def _merge(att, ml, proj, x2, wa, wm, wo, gf, tm):
    T = x2.shape[0]
    return pl.pallas_call(
        _merge_kernel,
        grid=(T // tm,),
        in_specs=[
            pl.BlockSpec((tm, ATT_WIDTH), lambda i: (i, 0)),
            pl.BlockSpec((tm, ML_WIDTH), lambda i: (i, 0)),
            pl.BlockSpec((tm, D_MODEL), lambda i: (i, 0)),
            pl.BlockSpec((tm, D_MODEL), lambda i: (i, 1)),
            pl.BlockSpec((tm, D_MODEL), lambda i: (i, 0)),
            _resident((ATT_WIDTH, D_MODEL), lambda i: (0, 0)),
            _resident((ML_WIDTH, D_MODEL), lambda i: (0, 0)),
            _resident((D_MODEL, D_MODEL), lambda i: (0, 0)),
            pl.BlockSpec((1, D_MODEL), lambda i: (0, 0)),
        ],
        out_specs=[
            pl.BlockSpec((tm, D_MODEL), lambda i: (i, 0)),
            pl.BlockSpec((D_MODEL, tm), lambda i: (0, i)),
        ],
        out_shape=[
            jax.ShapeDtypeStruct((T, D_MODEL), F32),
            jax.ShapeDtypeStruct((D_MODEL, T), BF16),
        ],
        compiler_params=_params("parallel"),
        name="merge_outproj",
    )(att, ml, proj, proj, x2, wa, wm, wo, gf)


_CAND_PAIRS = [(i, j) for i in range(PEER_TOPK) for j in range(PEER_TOPK) if (i + 1) * (j + 1) <= PEER_TOPK]
_N_CAND = len(_CAND_PAIRS)
_CAND_ROWS = -(-_N_CAND // SUBLANES) * SUBLANES
HEADS_IN_FLIGHT = 4


def _retrieve_kernel(xnt_ref, wq_ref, keys_ref, e1_ref, e2_ref, tau_ref,
                     sc_ref, a_ref, b_ref, cand_ref, cand2_ref):
    qpt = _dot(wq_ref[...], xnt_ref[...]).astype(BF16)
    for hc in range(2 * PEER_HEADS):
        sc_ref[hc] = _dot(keys_ref[hc], qpt[hc * PEER_KEY_DIM:(hc + 1) * PEER_KEY_DIM])

    def top_rows(e, dst_ref):
        n = N_KEYS // 4
        a, b, c, d = (e[i * n:(i + 1) * n] for i in range(4))
        a, b = jnp.maximum(a, b), jnp.minimum(a, b)
        c, d = jnp.maximum(c, d), jnp.minimum(c, d)
        a, c = jnp.maximum(a, c), jnp.minimum(a, c)
        b, d = jnp.maximum(b, d), jnp.minimum(b, d)
        b, c = jnp.maximum(b, c), jnp.minimum(b, c)
        lv = [a, b, c, d]
        for r in range(PEER_TOPK):
            m = jnp.max(lv[0], axis=0, keepdims=True)
            dst_ref[r:r + 1, :] = m
            hit = lv[0] == m
            lv = [jnp.where(hit, lv[i + 1], lv[i]) for i in range(3)] + [jnp.where(hit, 0.0, lv[3])]

    def head(h, a_ref, b_ref, cand_ref, cand2_ref):
        s1 = sc_ref[2 * h]
        s2 = sc_ref[2 * h + 1]
        ex1 = jnp.exp(s1 - jnp.max(s1, axis=0, keepdims=True))
        ex2 = jnp.exp(s2 - jnp.max(s2, axis=0, keepdims=True))
        top_rows(ex1, a_ref)
        top_rows(ex2, b_ref)
        cand_ref[...] = jnp.full_like(cand_ref, -1.0)
        for n, (i, j) in enumerate(_CAND_PAIRS):
            cand_ref[n:n + 1, :] = a_ref[i:i + 1, :] * b_ref[j:j + 1, :]
        cand = cand_ref[...]
        cur = cand
        for r in range(PEER_TOPK):
            tau = jnp.max(cur, axis=0, keepdims=True)
            cur = jnp.where(cur == tau, -1.0, cur)
        sel = cand >= jnp.maximum(tau, 0.0)
        z = jnp.sum(jnp.where(sel, cand, 0.0), axis=0, keepdims=True)
        rz = 1.0 / z
        cand2_ref[...] = jnp.full_like(cand2_ref, -1.0)
        for n, (i, j) in enumerate(_CAND_PAIRS):
            cand2_ref[n:n + 1, :] = (a_ref[i:i + 1, :] * rz) * b_ref[j:j + 1, :]
        tau2 = jnp.min(jnp.where(sel, cand2_ref[...], jnp.inf), axis=0, keepdims=True)
        e1_ref[h] = ex1 * rz
        e2_ref[h] = ex2
        tau_ref[pl.ds(h, 1), :] = tau2

    def head_group(g, carry):
        for slot in range(HEADS_IN_FLIGHT):
            head(g * HEADS_IN_FLIGHT + slot, a_ref.at[slot], b_ref.at[slot], cand_ref.at[slot], cand2_ref.at[slot])
        return carry

    lax.fori_loop(0, PEER_HEADS // HEADS_IN_FLIGHT, head_group, 0)


def _retrieve(xnt, wq_t, keys, tb):
    T = xnt.shape[1]
    qw = 2 * PEER_HEADS * PEER_KEY_DIM
    return pl.pallas_call(
        _retrieve_kernel,
        grid=(T // tb,),
        in_specs=[
            pl.BlockSpec((D_MODEL, tb), lambda i: (0, i)),
            _resident((qw, D_MODEL), lambda i: (0, 0)),
            _resident((2 * PEER_HEADS, N_KEYS, PEER_KEY_DIM), lambda i: (0, 0, 0)),
        ],
        out_specs=[
            pl.BlockSpec((PEER_HEADS, N_KEYS, tb), lambda i: (0, 0, i)),
            pl.BlockSpec((PEER_HEADS, N_KEYS, tb), lambda i: (0, 0, i)),
            pl.BlockSpec((PEER_HEADS, tb), lambda i: (0, i)),
        ],
        out_shape=[
            jax.ShapeDtypeStruct((PEER_HEADS, N_KEYS, T), F32),
            jax.ShapeDtypeStruct((PEER_HEADS, N_KEYS, T), F32),
            jax.ShapeDtypeStruct((PEER_HEADS, T), F32),
        ],
        scratch_shapes=[
            pltpu.VMEM((2 * PEER_HEADS, N_KEYS, tb), F32),
            pltpu.VMEM((HEADS_IN_FLIGHT, PEER_TOPK, tb), F32),
            pltpu.VMEM((HEADS_IN_FLIGHT, PEER_TOPK, tb), F32),
            pltpu.VMEM((HEADS_IN_FLIGHT, _CAND_ROWS, tb), F32),
            pltpu.VMEM((HEADS_IN_FLIGHT, _CAND_ROWS, tb), F32),
        ],
        compiler_params=_params("parallel"),
        name="peer_retrieve",
    )(xnt, wq_t, keys)


def _gelu(z):
    return 0.5 * z * (1.0 + lax.erf(z * (2.0 ** -0.5)))


def _experts_kernel(xnt_ref, u_ref, vt_ref, e1_ref, e2_ref, tau_ref, x1_ref, out_ref,
                    acc_ref, z_ref, w_ref, *, tsub):
    kb = pl.program_id(1)
    eb, tb = z_ref.shape

    @pl.when(kb == 0)
    def _():
        acc_ref[...] = jnp.zeros_like(acc_ref)

    z_ref[...] = _dot(u_ref[...], xnt_ref[...])
    for j in range(eb // N_KEYS):
        rs = slice(j * N_KEYS, (j + 1) * N_KEYS)
        for t in range(tb // tsub):
            ts = slice(t * tsub, (t + 1) * tsub)
            gates = jnp.zeros((N_KEYS, tsub), F32)
            for h in range(PEER_HEADS):
                p = e2_ref[h, :, ts] * e1_ref[h, j:j + 1, ts]
                gates = gates + jnp.where(p >= tau_ref[h:h + 1, ts], p, 0.0)
            w_ref[rs, ts] = (gates * _gelu(z_ref[rs, ts])).astype(BF16)
    acc_ref[...] += _dot(vt_ref[...], w_ref[...])

    @pl.when(kb == pl.num_programs(1) - 1)
    def _():
        out_ref[...] = x1_ref[...] + acc_ref[...].T


def _experts(xnt, u_bf, vt_bf, e1, e2, tau, x1, tb, eb, tsub):
    T = x1.shape[0]
    slabs = eb // N_KEYS
    return pl.pallas_call(
        functools.partial(_experts_kernel, tsub=tsub),
        grid=(T // tb, N_EXPERTS // eb),
        in_specs=[
            pl.BlockSpec((D_MODEL, tb), lambda i, k: (0, i)),
            pl.BlockSpec((eb, D_MODEL), lambda i, k: (k, 0)),
            pl.BlockSpec((D_MODEL, eb), lambda i, k: (0, k)),
            pl.BlockSpec((PEER_HEADS, slabs, tb), lambda i, k: (0, k, i)),
            pl.BlockSpec((PEER_HEADS, N_KEYS, tb), lambda i, k: (0, 0, i)),
            pl.BlockSpec((PEER_HEADS, tb), lambda i, k: (0, i)),
            pl.BlockSpec((tb, D_MODEL), lambda i, k: (i, 0)),
        ],
        out_specs=pl.BlockSpec((tb, D_MODEL), lambda i, k: (i, 0)),
        out_shape=jax.ShapeDtypeStruct((T, D_MODEL), F32),
        scratch_shapes=[
            pltpu.VMEM((D_MODEL, tb), F32),
            pltpu.VMEM((eb, tb), F32),
            pltpu.VMEM((eb, tb), BF16),
        ],
        compiler_params=_params("parallel", "arbitrary"),
        name="peer_experts",
    )(xnt, u_bf, vt_bf, e1, e2, tau, x1)


def _pick(n, pref):
    b = min(n, pref)
    while n % b:
        b //= 2
    return b


def _layer(x, norm_mix_g, w_in, b_in, conv_w, conv_b, q_norm_g, k_norm_g, sinks, ml_norm_g,
           w_proj_att, w_proj_ml, w_out, norm_ffn_g, w_peer_q, peer_keys, peer_u, peer_v):
    B, S, _ = x.shape
    T = B * S
    x2 = x.reshape(T, D_MODEL)

    segs = [(SRC_G, 2 * D_MODEL), (SRC_AQ, ATT_WIDTH), (SRC_MQ, ML_WIDTH), (SRC_MK, ML_WIDTH),
            (SRC_MV, ML_WIDTH), (SRC_MO, ML_WIDTH), (SRC_AK, ATT_KV_WIDTH), (SRC_AV, ATT_KV_WIDTH)]
    w_packed = jnp.concatenate([w_in[:, s:s + n] for s, n in segs], axis=1).astype(BF16)
    b_packed = jnp.concatenate([b_in[s:s + n] for s, n in segs])[None, :]
    n_if = 2 * ML_HEADS
    w_if = w_in[:, SRC_MI:SRC_MI + n_if]
    b_if = b_in[SRC_MI:SRC_MI + n_if]
    wif = jnp.pad(w_if, ((0, 0), (0, LANES - n_if))).astype(BF16)
    bif = jnp.pad(b_if, (0, LANES - n_if))[None, :]

    tm = _pick(T, 1024)
    proj, ifc, ift = _inproj(x2, norm_mix_g[None, :], w_packed, b_packed, wif, bif, tm, 512)

    att = _attention(proj, sinks, q_norm_g[None, :], k_norm_g[None, :], B, S)
    ml = _mlstm(proj, ifc, ift, conv_w, conv_b[None, :], ml_norm_g[None, :], B, S)

    x1, xnt = _merge(att, ml, proj, x2, w_proj_att.astype(BF16), w_proj_ml.astype(BF16),
                     w_out.astype(BF16), norm_ffn_g[None, :], _pick(T, 512))

    keys = peer_keys.reshape(2 * PEER_HEADS, N_KEYS, PEER_KEY_DIM).astype(BF16)
    e1, e2, tau = _retrieve(xnt, w_peer_q.T.astype(BF16), keys, _pick(T, 512))

    out = _experts(xnt, peer_u.astype(BF16), peer_v.T.astype(BF16), e1, e2, tau, x1,
                   _pick(T, 512), 1024, 256)
    return out.reshape(B, S, D_MODEL)


def kernel(x, norm_mix_g, w_in, b_in, conv_w, conv_b, q_norm_g, k_norm_g, sinks, ml_norm_g, w_proj_att, w_proj_ml, w_out, norm_ffn_g, w_peer_q, peer_keys, peer_u, peer_v):
    for l in range(norm_mix_g.shape[0]):
        x = _layer(x, norm_mix_g[l], w_in[l], b_in[l], conv_w[l], conv_b[l], q_norm_g[l], k_norm_g[l],
                   sinks[l], ml_norm_g[l], w_proj_att[l], w_proj_ml[l], w_out[l], norm_ffn_g[l],
                   w_peer_q[l], peer_keys[l], peer_u[l], peer_v[l])
    return x
```

```python
import functools

import jax
import jax.numpy as jnp
from jax import lax
from jax.experimental import pallas as pl
from jax.experimental.pallas import tpu as pltpu

F32 = jnp.float32
BF16 = jnp.bfloat16

D_MODEL = 2048
ATT_WIDTH = 1024
ATT_HEAD_DIM = 64
ATT_HEADS = 16
ATT_KV_HEADS = 4
ATT_GROUP = ATT_HEADS // ATT_KV_HEADS
ATT_KV_WIDTH = ATT_KV_HEADS * ATT_HEAD_DIM
ATT_BLOCK = 128
ML_WIDTH = 1024
ML_HEADS = 4
ML_HEAD_DIM = 256
CONV_WIDTH = 4
PEER_HEADS = 8
N_KEYS = 128
N_EXPERTS = N_KEYS * N_KEYS
PEER_KEY_DIM = 128
PEER_TOPK = 16
EPS = 1e-6
NEG_INF = -1e30

LANES = 128
SUBLANES = 8
VMEM_LIMIT = 56 * 1024 * 1024

COL_G = 0
COL_AQ = COL_G + 2 * D_MODEL
COL_MQ = COL_AQ + ATT_WIDTH
COL_MK = COL_MQ + ML_WIDTH
COL_MV = COL_MK + ML_WIDTH
COL_MO = COL_MV + ML_WIDTH
COL_AK = COL_MO + ML_WIDTH
COL_AV = COL_AK + ATT_KV_WIDTH
PACKED_WIDTH = COL_AV + ATT_KV_WIDTH
SRC_AQ = 0
SRC_AK = SRC_AQ + ATT_WIDTH
SRC_AV = SRC_AK + ATT_KV_WIDTH
SRC_MQ = SRC_AV + ATT_KV_WIDTH
SRC_MK = SRC_MQ + ML_WIDTH
SRC_MV = SRC_MK + ML_WIDTH
SRC_MO = SRC_MV + ML_WIDTH
SRC_MI = SRC_MO + ML_WIDTH
SRC_MF = SRC_MI + ML_HEADS
SRC_G = SRC_MF + ML_HEADS

ML_CHUNK = 256
ML_AUG = ML_HEAD_DIM + LANES


def _dot(a, b):
    return jnp.dot(a, b, preferred_element_type=F32)


def _dot_nt(a, b):
    return lax.dot_general(a, b, (((1,), (1,)), ((), ())), preferred_element_type=F32)


def _dot_tn(a, b):
    return lax.dot_general(a, b, (((0,), (0,)), ((), ())), preferred_element_type=F32)


def _split_bf16(a):
    hi = a.astype(BF16)
    lo = (a - hi.astype(F32)).astype(BF16)
    return hi, lo


def _params(*sem):
    return pltpu.CompilerParams(dimension_semantics=sem, vmem_limit_bytes=VMEM_LIMIT)


def _resident(shape, index_map):
    return pl.BlockSpec(shape, index_map, pipeline_mode=pl.Buffered(1))


def _inproj_kernel(x_ref, g_ref, w_ref, b_ref, wif_ref, bif_ref, o_ref, ifc_ref, ift_ref, xn_ref):
    @pl.when(pl.program_id(1) == 0)
    def _():
        x = x_ref[...]
        xn = x * lax.rsqrt(jnp.mean(x * x, axis=-1, keepdims=True) + EPS) * g_ref[...]
        xn_ref[...] = xn.astype(BF16)
        ifc = _dot(xn_ref[...], wif_ref[...]) + bif_ref[...]
        ifc_ref[...] = ifc
        ift_ref[...] = ifc.T[:SUBLANES]

    o_ref[...] = (_dot(xn_ref[...], w_ref[...]) + b_ref[...]).astype(o_ref.dtype)


def _inproj(x2, g, w_packed, b_packed, wif, bif, tm, tn):
    T = x2.shape[0]
    grid = (T // tm, PACKED_WIDTH // tn)
    return pl.pallas_call(
        _inproj_kernel,
        grid=grid,
        in_specs=[
            pl.BlockSpec((tm, D_MODEL), lambda i, j: (i, 0)),
            pl.BlockSpec((1, D_MODEL), lambda i, j: (0, 0)),
            pl.BlockSpec((D_MODEL, tn), lambda i, j: (0, j), pipeline_mode=pl.Buffered(1 if tn == PACKED_WIDTH else 2)),
            pl.BlockSpec((1, tn), lambda i, j: (0, j)),
            pl.BlockSpec((D_MODEL, LANES), lambda i, j: (0, 0)),
            pl.BlockSpec((1, LANES), lambda i, j: (0, 0)),
        ],
        out_specs=[
            pl.BlockSpec((tm, tn), lambda i, j: (i, j)),
            pl.BlockSpec((tm, LANES), lambda i, j: (i, 0)),
            pl.BlockSpec((SUBLANES, tm), lambda i, j: (0, i)),
        ],
        out_shape=[
            jax.ShapeDtypeStruct((T, PACKED_WIDTH), BF16),
            jax.ShapeDtypeStruct((T, LANES), F32),
            jax.ShapeDtypeStruct((SUBLANES, T), F32),
        ],
        scratch_shapes=[pltpu.VMEM((tm, D_MODEL), BF16)],
        compiler_params=_params("parallel", "arbitrary"),
        name="inproj",
    )(x2, g, w_packed, b_packed, wif, bif)


def _head_norm(a, g):
    return a * lax.rsqrt(jnp.mean(a * a, axis=-1, keepdims=True) + EPS) * g


def _attn_kernel(sinks_ref, q_ref, kc_ref, kp_ref, vc_ref, vp_ref, qg_ref, kg_ref, bdq_ref, bdk_ref, o_ref, *, nq):
    j = pl.program_id(1)
    blk = ATT_BLOCK
    rows = ATT_GROUP * blk
    qpos = lax.broadcasted_iota(jnp.int32, (rows, 2 * blk), 0) & (blk - 1)
    kpos = lax.broadcasted_iota(jnp.int32, (rows, 2 * blk), 1)
    is_prev = kpos < blk

    def window(no_prev):
        return jnp.where(is_prev, kpos - no_prev, qpos + blk) >= jnp.where(is_prev, qpos + 1, kpos)

    valid = [window(jnp.where(j > 0, 0, 2 * blk))] + [window(0)] * (nq - 1)
    inv_dim = 1.0 / ATT_HEAD_DIM
    q = q_ref[...].astype(F32)
    q_ms = _dot((q * q).astype(BF16), bdq_ref[...]) * inv_dim
    qn = (q * lax.rsqrt(q_ms + EPS) * (qg_ref[...] * ATT_HEAD_DIM ** -0.5)).astype(BF16)
    k = jnp.concatenate([kp_ref[...], kc_ref[...]], axis=0).astype(F32)
    k_ms = _dot((k * k).astype(BF16), bdk_ref[...]) * inv_dim
    kn = (k * lax.rsqrt(k_ms + EPS) * kg_ref[...]).astype(BF16)
    v = jnp.concatenate([vp_ref[...], vc_ref[...]], axis=0)
    ones = jnp.ones((2 * blk, ATT_HEAD_DIM), BF16)
    units = [(i, hk) for i in range(nq) for hk in range(ATT_KV_HEADS)]
    head_cols = lambda h: slice(h * ATT_HEAD_DIM, (h + 1) * ATT_HEAD_DIM)
    sinks = [jnp.concatenate([jnp.full((blk, 1), sinks_ref[h], F32)
                              for h in range(hk * ATT_GROUP, (hk + 1) * ATT_GROUP)], axis=0)
             for hk in range(ATT_KV_HEADS)]
    scores = []
    for i, hk in units:
        qs = jnp.concatenate([qn[i * blk:(i + 1) * blk, head_cols(h)]
                              for h in range(hk * ATT_GROUP, (hk + 1) * ATT_GROUP)], axis=0)
        scores.append(jnp.where(valid[i], _dot_nt(qs, kn[i * blk:(i + 2) * blk, head_cols(hk)]), NEG_INF))
    maxes = [jnp.maximum(jnp.max(s, axis=-1, keepdims=True), sinks[hk]) for s, (i, hk) in zip(scores, units)]
    probs = [jnp.exp(s - m).astype(BF16) for s, m in zip(scores, maxes)]
    outs = [[] for _ in range(nq)]
    for n, (i, hk) in enumerate(units):
        den = _dot(probs[n], ones) + jnp.exp(sinks[hk] - maxes[n])
        r = _dot(probs[n], v[i * blk:(i + 2) * blk, head_cols(hk)]) / den
        outs[i] += [r[g * blk:(g + 1) * blk] for g in range(ATT_GROUP)]
    o_ref[...] = jnp.concatenate([jnp.concatenate(o, axis=-1) for o in outs], axis=0).astype(o_ref.dtype)


def _block_ones(width, block):
    idx = jnp.arange(width) // block
    return (idx[:, None] == idx[None, :]).astype(BF16)


def _attention(proj, sinks, qg, kg, B, S):
    nb = S // ATT_BLOCK
    nq = 2 if nb % 2 == 0 else 1
    T = B * S
    kcol = COL_AK // ATT_KV_WIDTH
    vcol = COL_AV // ATT_KV_WIDTH
    steps = nb // nq
    cur = lambda b, j: b * steps + j
    prev = lambda b, j: b * nb + jnp.maximum(nq * j - 1, 0)
    return pl.pallas_call(
        functools.partial(_attn_kernel, nq=nq),
        grid=(B, steps),
        in_specs=[
            pl.BlockSpec(memory_space=pltpu.SMEM),
            pl.BlockSpec((nq * ATT_BLOCK, ATT_WIDTH), lambda b, j: (cur(b, j), COL_AQ // ATT_WIDTH)),
            pl.BlockSpec((nq * ATT_BLOCK, ATT_KV_WIDTH), lambda b, j: (cur(b, j), kcol)),
            pl.BlockSpec((ATT_BLOCK, ATT_KV_WIDTH), lambda b, j: (prev(b, j), kcol)),
            pl.BlockSpec((nq * ATT_BLOCK, ATT_KV_WIDTH), lambda b, j: (cur(b, j), vcol)),
            pl.BlockSpec((ATT_BLOCK, ATT_KV_WIDTH), lambda b, j: (prev(b, j), vcol)),
            pl.BlockSpec((1, ATT_WIDTH), lambda b, j: (0, 0)),
            pl.BlockSpec((1, ATT_KV_WIDTH), lambda b, j: (0, 0)),
            _resident((ATT_WIDTH, ATT_WIDTH), lambda b, j: (0, 0)),
            _resident((ATT_KV_WIDTH, ATT_KV_WIDTH), lambda b, j: (0, 0)),
        ],
        out_specs=pl.BlockSpec((nq * ATT_BLOCK, ATT_WIDTH), lambda b, j: (cur(b, j), 0)),
        out_shape=jax.ShapeDtypeStruct((T, ATT_WIDTH), BF16),
        compiler_params=_params("parallel", "arbitrary"),
        name="swa_attention",
    )(sinks, proj, proj, proj, proj, proj,
      jnp.tile(qg, (1, ATT_HEADS)), jnp.tile(kg, (1, ATT_KV_HEADS)),
      _block_ones(ATT_WIDTH, ATT_HEAD_DIM), _block_ones(ATT_KV_WIDTH, ATT_HEAD_DIM))


def _shift_rows(cur, tail, j):
    cur_r = pltpu.roll(cur, j, 0)
    tail_r = pltpu.roll(tail, j, 0)
    row = lax.broadcasted_iota(jnp.int32, tail.shape, 0)
    first = jnp.where(row < j, tail_r, cur_r[:SUBLANES])
    return jnp.concatenate([first, cur_r[SUBLANES:]], axis=0)


def _conv_silu(cur, tail, w, b):
    y = cur * w[CONV_WIDTH - 1:CONV_WIDTH] + b
    for j in range(1, CONV_WIDTH):
        y = y + _shift_rows(cur, tail, j) * w[CONV_WIDTH - 1 - j:CONV_WIDTH - j]
    return y / (1.0 + jnp.exp(-y))


def _mlstm_kernel(q_ref, k_ref, v_ref, o_ref, ifc_ref, ift_ref, cw_ref, cb_ref, ng_ref,
                  out_ref, ct_ref, m_ref, qtail_ref, ktail_ref):
    L = ML_CHUNK

    @pl.when(pl.program_id(1) == 0)
    def _():
        ct_ref[...] = jnp.zeros_like(ct_ref)
        m_ref[...] = jnp.full_like(m_ref, NEG_INF)
        qtail_ref[...] = jnp.zeros_like(qtail_ref)
        ktail_ref[...] = jnp.zeros_like(ktail_ref)

    q_raw = q_ref[...].astype(F32)
    k_raw = k_ref[...].astype(F32)
    cw = cw_ref[...]
    cb = cb_ref[...]
    q_all = _conv_silu(q_raw, qtail_ref[...], cw[:, :ML_WIDTH], cb[:, :ML_WIDTH])
    k_all = _conv_silu(k_raw, ktail_ref[...], cw[:, ML_WIDTH:], cb[:, ML_WIDTH:]) * (ML_HEAD_DIM ** -0.5)
    qtail_ref[...] = q_raw[L - SUBLANES:]
    ktail_ref[...] = k_raw[L - SUBLANES:]

    def log_sigmoid(a):
        return jnp.minimum(a, 0.0) - jnp.log1p(jnp.exp(-jnp.abs(a)))

    ifc = ifc_ref[...]
    ift = ift_ref[...]
    r = lax.broadcasted_iota(jnp.int32, (L, L), 0)
    c = lax.broadcasted_iota(jnp.int32, (L, L), 1)
    causal = c <= r
    lower = causal.astype(BF16)
    upper = (r <= c).astype(BF16)
    lh, ll = _split_bf16(log_sigmoid(ifc))
    b_cols = _dot(lower, lh) + _dot(lower, ll)
    th, tl = _split_bf16(log_sigmoid(ift))
    b_rows = _dot(th, upper) + _dot(tl, upper)

    lane = lax.broadcasted_iota(jnp.int32, (L, LANES), 1)
    ones_col = (lane == 0).astype(BF16)
    ng = ng_ref[...]
    H = range(ML_HEADS)
    hs = [slice(h * ML_HEAD_DIM, (h + 1) * ML_HEAD_DIM) for h in H]
    i_row = [ift[h:h + 1, :] for h in H]
    i_col = [ifc[:, h:h + 1] for h in H]
    b_row = [b_rows[ML_HEADS + h:ML_HEADS + h + 1, :] for h in H]
    b_col = [b_cols[:, ML_HEADS + h:ML_HEADS + h + 1] for h in H]
    m_prev = [m_ref[h, 0:1, 0:1] for h in H]
    q = [q_all[:, hs[h]].astype(BF16) for h in H]
    k = [k_all[:, hs[h]] for h in H]
    v_aug = [jnp.concatenate([v_ref[:, hs[h]], ones_col], axis=-1) for h in H]

    log_d = [jnp.where(causal, b_col[h] - b_row[h] + i_row[h], NEG_INF) for h in H]
    m_inter = [b_col[h] + m_prev[h] for h in H]
    m_t = [jnp.maximum(jnp.max(log_d[h], axis=-1, keepdims=True), m_inter[h]) for h in H]
    qk = [_dot_nt(q[h], k[h].astype(BF16)) for h in H]
    inter = [_dot(q[h], ct_ref[h].astype(BF16)) for h in H]
    sc = [(qk[h] * jnp.exp(log_d[h] - m_t[h])).astype(BF16) for h in H]
    nd = [_dot(sc[h], v_aug[h]) + jnp.exp(m_inter[h] - m_t[h]) * inter[h] for h in H]
    hh = [nd[h][:, :ML_HEAD_DIM]
          / jnp.maximum(jnp.abs(nd[h][:, ML_HEAD_DIM:ML_HEAD_DIM + 1]), jnp.exp(-m_t[h])) for h in H]

    b_last = [b_col[h][L - 1:L, :] for h in H]
    log_w = [b_last[h] - b_col[h] + i_col[h] for h in H]
    m_new = [jnp.maximum(b_last[h] + m_prev[h], jnp.max(log_w[h], axis=0, keepdims=True)) for h in H]
    kw = [(k[h] * jnp.exp(log_w[h] - m_new[h])).astype(BF16) for h in H]
    for h in H:
        ct_ref[h] = jnp.exp(b_last[h] + m_prev[h] - m_new[h]) * ct_ref[h] + _dot_tn(kw[h], v_aug[h])
        m_ref[h] = jnp.broadcast_to(m_new[h], m_ref.shape[1:])

    outs = [_head_norm(hh[h], ng[:, hs[h]]) / (1.0 + jnp.exp(-o_ref[:, hs[h]].astype(F32))) for h in H]
    out_ref[...] = jnp.concatenate(outs, axis=-1).astype(out_ref.dtype)


def _mlstm(proj, ifc, ift, conv_w, conv_b, ml_norm_g, B, S):
    L = ML_CHUNK
    nc = S // L
    T = B * S
    row = lambda b, c: b * nc + c
    seg = lambda col: pl.BlockSpec((L, ML_WIDTH), lambda b, c, col=col: (row(b, c), col // ML_WIDTH))
    return pl.pallas_call(
        _mlstm_kernel,
        grid=(B, nc),
        in_specs=[
            seg(COL_MQ), seg(COL_MK), seg(COL_MV), seg(COL_MO),
            pl.BlockSpec((L, LANES), lambda b, c: (row(b, c), 0)),
            pl.BlockSpec((SUBLANES, L), lambda b, c: (0, row(b, c))),
            pl.BlockSpec((CONV_WIDTH, 2 * ML_WIDTH), lambda b, c: (0, 0)),
            pl.BlockSpec((1, 2 * ML_WIDTH), lambda b, c: (0, 0)),
            pl.BlockSpec((1, ML_WIDTH), lambda b, c: (0, 0)),
        ],
        out_specs=pl.BlockSpec((L, ML_WIDTH), lambda b, c: (row(b, c), 0)),
        out_shape=jax.ShapeDtypeStruct((T, ML_WIDTH), BF16),
        scratch_shapes=[
            pltpu.VMEM((ML_HEADS, ML_HEAD_DIM, ML_AUG), F32),
            pltpu.VMEM((ML_HEADS, SUBLANES, LANES), F32),
            pltpu.VMEM((SUBLANES, ML_WIDTH), F32),
            pltpu.VMEM((SUBLANES, ML_WIDTH), F32),
        ],
        compiler_params=_params("parallel", "arbitrary"),
        name="mlstm",
    )(proj, proj, proj, proj, ifc, ift, conv_w, conv_b, ml_norm_g)


def _sigmoid(a):
    return 1.0 / (1.0 + jnp.exp(-a))


def _merge_kernel(att_ref, ml_ref, g0_ref, g1_ref, x_ref, wa_ref, wm_ref, wo_ref, gf_ref,
                  x1_ref, xnt_ref):
    a = _dot(att_ref[...], wa_ref[...])
    m = _dot(ml_ref[...], wm_ref[...])
    merged = _sigmoid(g0_ref[...].astype(F32)) * a + _sigmoid(g1_ref[...].astype(F32)) * m
    x1 = x_ref[...] + _dot(merged.astype(BF16), wo_ref[...])
    x1_ref[...] = x1
    xn = x1 * lax.rsqrt(jnp.mean(x1 * x1, axis=-1, keepdims=True) + EPS) * gf_ref[...]
    xnt_ref[...] = xn.T.astype(BF16)


def _merge(att, ml, proj, x2, wa, wm, wo, gf, tm):
    T = x2.shape[0]
    return pl.pallas_call(
        _merge_kernel,
        grid=(T // tm,),
        in_specs=[
            pl.BlockSpec((tm, ATT_WIDTH), lambda i: (i, 0)),
            pl.BlockSpec((tm, ML_WIDTH), lambda i: (i, 0)),
            pl.BlockSpec((tm, D_MODEL), lambda i: (i, 0)),
            pl.BlockSpec((tm, D_MODEL), lambda i: (i, 1)),
            pl.BlockSpec((tm, D_MODEL), lambda i: (i, 0)),
            _resident((ATT_WIDTH, D_MODEL), lambda i: (0, 0)),
            _resident((ML_WIDTH, D_MODEL), lambda i: (0, 0)),
            _resident((D_MODEL, D_MODEL), lambda i: (0, 0)),
            pl.BlockSpec((1, D_MODEL), lambda i: (0, 0)),
        ],
        out_specs=[
            pl.BlockSpec((tm, D_MODEL), lambda i: (i, 0)),
            pl.BlockSpec((D_MODEL, tm), lambda i: (0, i)),
        ],
        out_shape=[
            jax.ShapeDtypeStruct((T, D_MODEL), F32),
            jax.ShapeDtypeStruct((D_MODEL, T), BF16),
        ],
        compiler_params=_params("parallel"),
        name="merge_outproj",
    )(att, ml, proj, proj, x2, wa, wm, wo, gf)


_CAND_PAIRS = [(i, j) for i in range(PEER_TOPK) for j in range(PEER_TOPK) if (i + 1) * (j + 1) <= PEER_TOPK]
_N_CAND = len(_CAND_PAIRS)
_CAND_ROWS = -(-_N_CAND // SUBLANES) * SUBLANES
HEADS_IN_FLIGHT = 4


def _retrieve_kernel(xnt_ref, wq_ref, keys_ref, e1_ref, e2_ref, tau_ref,
                     sc_ref, a_ref, b_ref, cand_ref, cand2_ref):
    qpt = _dot(wq_ref[...], xnt_ref[...]).astype(BF16)
    for hc in range(2 * PEER_HEADS):
        sc_ref[hc] = _dot(keys_ref[hc], qpt[hc * PEER_KEY_DIM:(hc + 1) * PEER_KEY_DIM])

    def top_rows(e, dst_ref):
        n = N_KEYS // 4
        a, b, c, d = (e[i * n:(i + 1) * n] for i in range(4))
        a, b = jnp.maximum(a, b), jnp.minimum(a, b)
        c, d = jnp.maximum(c, d), jnp.minimum(c, d)
        a, c = jnp.maximum(a, c), jnp.minimum(a, c)
        b, d = jnp.maximum(b, d), jnp.minimum(b, d)
        b, c = jnp.maximum(b, c), jnp.minimum(b, c)
        lv = [a, b, c, d]
        for r in range(PEER_TOPK):
            m = jnp.max(lv[0], axis=0, keepdims=True)
            dst_ref[r:r + 1, :] = m
            hit = lv[0] == m
            lv = [jnp.where(hit, lv[i + 1], lv[i]) for i in range(3)] + [jnp.where(hit, 0.0, lv[3])]

    def head(h, a_ref, b_ref, cand_ref, cand2_ref):
        s1 = sc_ref[2 * h]
        s2 = sc_ref[2 * h + 1]
        ex1 = jnp.exp(s1 - jnp.max(s1, axis=0, keepdims=True))
        ex2 = jnp.exp(s2 - jnp.max(s2, axis=0, keepdims=True))
        top_rows(ex1, a_ref)
        top_rows(ex2, b_ref)
        cand_ref[...] = jnp.full_like(cand_ref, -1.0)
        for n, (i, j) in enumerate(_CAND_PAIRS):
            cand_ref[n:n + 1, :] = a_ref[i:i + 1, :] * b_ref[j:j + 1, :]
        cand = cand_ref[...]
        cur = cand
        for r in range(PEER_TOPK):
            tau = jnp.max(cur, axis=0, keepdims=True)
            cur = jnp.where(cur == tau, -1.0, cur)
        sel = cand >= jnp.maximum(tau, 0.0)
        z = jnp.sum(jnp.where(sel, cand, 0.0), axis=0, keepdims=True)
        rz = 1.0 / z
        cand2_ref[...] = jnp.full_like(cand2_ref, -1.0)
        for n, (i, j) in enumerate(_CAND_PAIRS):
            cand2_ref[n:n + 1, :] = (a_ref[i:i + 1, :] * rz) * b_ref[j:j + 1, :]
        tau2 = jnp.min(jnp.where(sel, cand2_ref[...], jnp.inf), axis=0, keepdims=True)
        e1_ref[h] = ex1 * rz
        e2_ref[h] = ex2
        tau_ref[pl.ds(h, 1), :] = tau2

    def head_group(g, carry):
        for slot in range(HEADS_IN_FLIGHT):
            head(g * HEADS_IN_FLIGHT + slot, a_ref.at[slot], b_ref.at[slot], cand_ref.at[slot], cand2_ref.at[slot])
        return carry

    lax.fori_loop(0, PEER_HEADS // HEADS_IN_FLIGHT, head_group, 0)


def _retrieve(xnt, wq_t, keys, tb):
    T = xnt.shape[1]
    qw = 2 * PEER_HEADS * PEER_KEY_DIM
    return pl.pallas_call(
        _retrieve_kernel,
        grid=(T // tb,),
        in_specs=[
            pl.BlockSpec((D_MODEL, tb), lambda i: (0, i)),
            _resident((qw, D_MODEL), lambda i: (0, 0)),
            _resident((2 * PEER_HEADS, N_KEYS, PEER_KEY_DIM), lambda i: (0, 0, 0)),
        ],
        out_specs=[
            pl.BlockSpec((PEER_HEADS, N_KEYS, tb), lambda i: (0, 0, i)),
            pl.BlockSpec((PEER_HEADS, N_KEYS, tb), lambda i: (0, 0, i)),
            pl.BlockSpec((PEER_HEADS, tb), lambda i: (0, i)),
        ],
        out_shape=[
            jax.ShapeDtypeStruct((PEER_HEADS, N_KEYS, T), F32),
            jax.ShapeDtypeStruct((PEER_HEADS, N_KEYS, T), F32),
            jax.ShapeDtypeStruct((PEER_HEADS, T), F32),
        ],
        scratch_shapes=[
            pltpu.VMEM((2 * PEER_HEADS, N_KEYS, tb), F32),
            pltpu.VMEM((HEADS_IN_FLIGHT, PEER_TOPK, tb), F32),
            pltpu.VMEM((HEADS_IN_FLIGHT, PEER_TOPK, tb), F32),
            pltpu.VMEM((HEADS_IN_FLIGHT, _CAND_ROWS, tb), F32),
            pltpu.VMEM((HEADS_IN_FLIGHT, _CAND_ROWS, tb), F32),
        ],
        compiler_params=_params("parallel"),
        name="peer_retrieve",
    )(xnt, wq_t, keys)


def _gelu(z):
    return 0.5 * z * (1.0 + lax.erf(z * (2.0 ** -0.5)))


def _experts_kernel(xnt_ref, u_ref, vt_ref, e1_ref, e2_ref, tau_ref, x1_ref, out_ref,
                    acc_ref, z_ref, w_ref, *, tsub):
    kb = pl.program_id(1)
    eb, tb = z_ref.shape

    @pl.when(kb == 0)
    def _():
        acc_ref[...] = jnp.zeros_like(acc_ref)

    z_ref[...] = _dot(u_ref[...], xnt_ref[...])
    for j in range(eb // N_KEYS):
        rs = slice(j * N_KEYS, (j + 1) * N_KEYS)
        for t in range(tb // tsub):
            ts = slice(t * tsub, (t + 1) * tsub)
            gates = jnp.zeros((N_KEYS, tsub), F32)
            for h in range(PEER_HEADS):
                p = e2_ref[h, :, ts] * e1_ref[h, j:j + 1, ts]
                gates = gates + jnp.where(p >= tau_ref[h:h + 1, ts], p, 0.0)
            w_ref[rs, ts] = (gates * _gelu(z_ref[rs, ts])).astype(BF16)
    acc_ref[...] += _dot(vt_ref[...], w_ref[...])

    @pl.when(kb == pl.num_programs(1) - 1)
    def _():
        out_ref[...] = x1_ref[...] + acc_ref[...].T


def _experts(xnt, u_bf, vt_bf, e1, e2, tau, x1, tb, eb, tsub):
    T = x1.shape[0]
    slabs = eb // N_KEYS
    return pl.pallas_call(
        functools.partial(_experts_kernel, tsub=tsub),
        grid=(T // tb, N_EXPERTS // eb),
        in_specs=[
            pl.BlockSpec((D_MODEL, tb), lambda i, k: (0, i)),
            pl.BlockSpec((eb, D_MODEL), lambda i, k: (k, 0)),
            pl.BlockSpec((D_MODEL, eb), lambda i, k: (0, k)),
            pl.BlockSpec((PEER_HEADS, slabs, tb), lambda i, k: (0, k, i)),
            pl.BlockSpec((PEER_HEADS, N_KEYS, tb), lambda i, k: (0, 0, i)),
            pl.BlockSpec((PEER_HEADS, tb), lambda i, k: (0, i)),
            pl.BlockSpec((tb, D_MODEL), lambda i, k: (i, 0)),
        ],
        out_specs=pl.BlockSpec((tb, D_MODEL), lambda i, k: (i, 0)),
        out_shape=jax.ShapeDtypeStruct((T, D_MODEL), F32),
        scratch_shapes=[
            pltpu.VMEM((D_MODEL, tb), F32),
            pltpu.VMEM((eb, tb), F32),
            pltpu.VMEM((eb, tb), BF16),
        ],
        compiler_params=_params("parallel", "arbitrary"),
        name="peer_experts",
    )(xnt, u_bf, vt_bf, e1, e2, tau, x1)


def _pick(n, pref):
    b = min(n, pref)
    while n % b:
        b //= 2
    return b


def _layer(x, norm_mix_g, w_in, b_in, conv_w, conv_b, q_norm_g, k_norm_g, sinks, ml_norm_g,
           w_proj_att, w_proj_ml, w_out, norm_ffn_g, w_peer_q, peer_keys, peer_u, peer_v):
    B, S, _ = x.shape
    T = B * S
    x2 = x.reshape(T, D_MODEL)

    segs = [(SRC_G, 2 * D_MODEL), (SRC_AQ, ATT_WIDTH), (SRC_MQ, ML_WIDTH), (SRC_MK, ML_WIDTH),
            (SRC_MV, ML_WIDTH), (SRC_MO, ML_WIDTH), (SRC_AK, ATT_KV_WIDTH), (SRC_AV, ATT_KV_WIDTH)]
    w_packed = jnp.concatenate([w_in[:, s:s + n] for s, n in segs], axis=1).astype(BF16)
    b_packed = jnp.concatenate([b_in[s:s + n] for s, n in segs])[None, :]
    n_if = 2 * ML_HEADS
    w_if = w_in[:, SRC_MI:SRC_MI + n_if]
    b_if = b_in[SRC_MI:SRC_MI + n_if]
    wif = jnp.pad(w_if, ((0, 0), (0, LANES - n_if))).astype(BF16)
    bif = jnp.pad(b_if, (0, LANES - n_if))[None, :]

    tm = _pick(T, 256)
    proj, ifc, ift = _inproj(x2, norm_mix_g[None, :], w_packed, b_packed, wif, bif, tm, PACKED_WIDTH)

    att = _attention(proj, sinks, q_norm_g[None, :], k_norm_g[None, :], B, S)
    ml = _mlstm(proj, ifc, ift, conv_w, conv_b[None, :], ml_norm_g[None, :], B, S)

    x1, xnt = _merge(att, ml, proj, x2, w_proj_att.astype(BF16), w_proj_ml.astype(BF16),
                     w_out.astype(BF16), norm_ffn_g[None, :], _pick(T, 512))

    keys = peer_keys.reshape(2 * PEER_HEADS, N_KEYS, PEER_KEY_DIM).astype(BF16)
    e1, e2, tau = _retrieve(xnt, w_peer_q.T.astype(BF16), keys, _pick(T, 512))

    out = _experts(xnt, peer_u.astype(BF16), peer_v.T.astype(BF16), e1, e2, tau, x1,
                   _pick(T, 512), 1024, 256)
    return out.reshape(B, S, D_MODEL)


def kernel(x, norm_mix_g, w_in, b_in, conv_w, conv_b, q_norm_g, k_norm_g, sinks, ml_norm_g, w_proj_att, w_proj_ml, w_out, norm_ffn_g, w_peer_q, peer_keys, peer_u, peer_v):
    for l in range(norm_mix_g.shape[0]):
        x = _layer(x, norm_mix_g[l], w_in[l], b_in[l], conv_w[l], conv_b[l], q_norm_g[l], k_norm_g[l],
                   sinks[l], ml_norm_g[l], w_proj_att[l], w_proj_ml[l], w_out[l], norm_ffn_g[l],
                   w_peer_q[l], peer_keys[l], peer_u[l], peer_v[l])
    return x
```
